```python
import jax
import jax.numpy as jnp
from jax import lax
import numpy as np

D_MODEL = 1024
BATCH = 8
SEQ = 4096
DEPTH = 1
DEC_BATCH = 32
DEC_SEQ = 4
PAST_LEN = 16384
PAGE_SIZE = 128

ATTN_HEADS = 8
HEAD_DIM = 64
ATTN_WIDTH = ATTN_HEADS * HEAD_DIM
CONV_CHANNELS = D_MODEL - ATTN_WIDTH
MIX_WIDTH = ATTN_WIDTH + CONV_CHANNELS
IN_WIDTH = 3 * ATTN_WIDTH + 2 * CONV_CHANNELS
CONV_K = 31
MOBA_BLOCK = 256
MOBA_TOPK = 3
Q_CHUNK = 16
PEER_HEADS = 8
N_KEYS = 128
N_EXPERTS = N_KEYS * N_KEYS
PEER_TOPK = 16
D_KEY = 256
HALF_KEY = D_KEY // 2
TOK_CHUNK = 256
EPS = 1e-6

kernel_name = 'hymba_conformer_moba_peer_step'


def rms_norm(x, g):
    xf = x.astype(jnp.float32)
    xf = xf * lax.rsqrt(jnp.mean(xf * xf, axis=-1, keepdims=True) + EPS)
    return xf.astype(x.dtype) * g


def attend(q, k_sel, v_sel, sel_ok, k_loc, v_loc, loc_ok):
    scale = HEAD_DIM ** -0.5
    s_loc = jnp.einsum('bthd,blhd->bthl', q, k_loc, preferred_element_type=jnp.float32) * scale
    s_loc = jnp.where(loc_ok[None, :, None, :], s_loc, -jnp.inf)
    if k_sel is None:
        p = jax.nn.softmax(s_loc, axis=-1).astype(v_loc.dtype)
        return jnp.einsum('bthl,blhd->bthd', p, v_loc)
    s_sel = jnp.einsum('bthd,bthnd->bthn', q, k_sel, preferred_element_type=jnp.float32) * scale
    if sel_ok is not None:
        s_sel = jnp.where(sel_ok[None, :, None, :], s_sel, -jnp.inf)
    n_s = s_sel.shape[-1]
    p = jax.nn.softmax(jnp.concatenate([s_sel, s_loc], axis=-1), axis=-1).astype(v_loc.dtype)
    return (jnp.einsum('bthn,bthnd->bthd', p[..., :n_s], v_sel)
            + jnp.einsum('bthl,blhd->bthd', p[..., n_s:], v_loc))


def moba_prompt(q, k, v):
    b, s, h, dh = q.shape
    nb_full = s // MOBA_BLOCK
    n_sel = min(MOBA_TOPK, (s - 1) // MOBA_BLOCK)
    s_pad = -(-s // MOBA_BLOCK) * MOBA_BLOCK
    pad = ((0, 0), (0, s_pad - s), (0, 0), (0, 0))
    k_pad = jnp.pad(k, pad)
    v_pad = jnp.pad(v, pad)
    q_blk = jnp.arange(s) // MOBA_BLOCK
    if n_sel > 0:
        k_blocks = k_pad[:, :nb_full * MOBA_BLOCK].reshape(b, nb_full, MOBA_BLOCK, h, dh)
        v_blocks = v_pad[:, :nb_full * MOBA_BLOCK].reshape(b, nb_full, MOBA_BLOCK, h, dh)
        k_mean = jnp.mean(k_blocks, axis=2, dtype=jnp.float32)
        gate = jnp.einsum('bshd,bnhd->bshn', q.astype(jnp.float32), k_mean)
        fully_past = jnp.arange(nb_full)[None, :] < q_blk[:, None]
        gate = jnp.where(fully_past[None, :, None, :], gate, -jnp.inf)
        _, sel = lax.top_k(gate, n_sel)
        sel_ok = jnp.arange(n_sel)[None, :] < q_blk[:, None]
        b_idx = jnp.arange(b)[:, None, None, None]
        h_idx = jnp.arange(h)[None, None, :, None]

    def one_chunk(c):
        t0 = c * Q_CHUNK
        blk0 = (t0 // MOBA_BLOCK) * MOBA_BLOCK
        q_c = lax.dynamic_slice_in_dim(q, t0, Q_CHUNK, axis=1)
        k_loc = lax.dynamic_slice_in_dim(k_pad, blk0, MOBA_BLOCK, axis=1)
        v_loc = lax.dynamic_slice_in_dim(v_pad, blk0, MOBA_BLOCK, axis=1)
        loc_ok = (blk0 + jnp.arange(MOBA_BLOCK))[None, :] <= (t0 + jnp.arange(Q_CHUNK))[:, None]
        if n_sel == 0:
            return attend(q_c, None, None, None, k_loc, v_loc, loc_ok)
        sel_c = lax.dynamic_slice_in_dim(sel, t0, Q_CHUNK, axis=1)
        k_sel = k_blocks[b_idx, sel_c, :, h_idx].reshape(b, Q_CHUNK, h, n_sel * MOBA_BLOCK, dh)
        v_sel = v_blocks[b_idx, sel_c, :, h_idx].reshape(b, Q_CHUNK, h, n_sel * MOBA_BLOCK, dh)
        ok_c = jnp.repeat(lax.dynamic_slice_in_dim(sel_ok, t0, Q_CHUNK, axis=0), MOBA_BLOCK, axis=1)
        return attend(q_c, k_sel, v_sel, ok_c, k_loc, v_loc, loc_ok)

    out = lax.map(one_chunk, jnp.arange(s // Q_CHUNK))
    return jnp.moveaxis(out, 0, 1).reshape(b, s, h, dh)


def moba_sample(q, k_new, v_new, k_pool, v_pool, page_table):
    db, t, h, dh = q.shape
    n_pages = page_table.shape[1]
    past = n_pages * PAGE_SIZE
    ppb = MOBA_BLOCK // PAGE_SIZE
    nb_past = past // MOBA_BLOCK
    n_sel = min(MOBA_TOPK, nb_past)
    own_page0 = nb_past * ppb
    r = past - nb_past * MOBA_BLOCK
    own_pages = page_table[:, own_page0:]
    k_loc = jnp.concatenate([k_pool[own_pages].reshape(db, r, h, dh), k_new], axis=1)
    v_loc = jnp.concatenate([v_pool[own_pages].reshape(db, r, h, dh), v_new], axis=1)
    loc_ok = jnp.concatenate([jnp.ones((t, r), dtype=bool), jnp.tril(jnp.ones((t, t), dtype=bool))], axis=1)
    if n_sel == 0:
        return attend(q, None, None, None, k_loc, v_loc, loc_ok)
    page_mean = jnp.mean(k_pool, axis=1, dtype=jnp.float32)
    pm_seq = page_mean[page_table[:, :own_page0]]
    k_mean = pm_seq.reshape(db, nb_past, ppb, h, dh).mean(axis=2)
    gate = jnp.einsum('bthd,bnhd->bthn', q.astype(jnp.float32), k_mean)
    _, sel = lax.top_k(gate, n_sel)
    logical = sel[..., None] * ppb + jnp.arange(ppb)
    phys = page_table[jnp.arange(db)[:, None, None, None, None], logical]
    h_idx = jnp.arange(h)[None, None, :, None, None]
    k_sel = k_pool[phys, :, h_idx].reshape(db, t, h, n_sel * MOBA_BLOCK, dh)
    v_sel = v_pool[phys, :, h_idx].reshape(db, t, h, n_sel * MOBA_BLOCK, dh)
    return attend(q, k_sel, v_sel, None, k_loc, v_loc, loc_ok)


def peer_ffn(x, w_query, keys1, keys2, expert_u, expert_v):
    lead = x.shape[:-1]
    xt = x.reshape(-1, D_MODEL)
    n = xt.shape[0]
    chunk = min(TOK_CHUNK, n)
    n_chunks = -(-n // chunk)
    xt = jnp.pad(xt, ((0, n_chunks * chunk - n), (0, 0)))

    def one(xc):
        qh = (xc @ w_query).astype(jnp.float32).reshape(chunk, PEER_HEADS, D_KEY)
        s1 = jnp.einsum('nhk,hmk->nhm', qh[..., :HALF_KEY], keys1.astype(jnp.float32))
        s2 = jnp.einsum('nhk,hmk->nhm', qh[..., HALF_KEY:], keys2.astype(jnp.float32))
        v1, i1 = lax.top_k(s1, PEER_TOPK)
        v2, i2 = lax.top_k(s2, PEER_TOPK)
        cand = (v1[..., :, None] + v2[..., None, :]).reshape(chunk, PEER_HEADS, PEER_TOPK * PEER_TOPK)
        cidx = (i1[..., :, None] * N_KEYS + i2[..., None, :]).reshape(chunk, PEER_HEADS, PEER_TOPK * PEER_TOPK)
        top, pos = lax.top_k(cand, PEER_TOPK)
        idx = jnp.take_along_axis(cidx, pos, axis=-1)
        g = jax.nn.softmax(top, axis=-1).astype(xc.dtype)
        act = jax.nn.gelu(jnp.einsum('nd,nhkd->nhk', xc, expert_u[idx]), approximate=False)
        return jnp.einsum('nhk,nhkd->nd', g * act, expert_v[idx])

    out = lax.map(one, xt.reshape(n_chunks, chunk, D_MODEL))
    return out.reshape(-1, D_MODEL)[:n].reshape(lead + (D_MODEL,))


def layer_front(x, norm1_g, w_in, q_norm_g, k_norm_g):
    xn = rms_norm(x, norm1_g)
    proj = xn @ w_in
    q, k, v, a, gt = jnp.split(proj, [ATTN_WIDTH, 2 * ATTN_WIDTH, 3 * ATTN_WIDTH, 3 * ATTN_WIDTH + CONV_CHANNELS], axis=-1)
    hs = x.shape[:-1] + (ATTN_HEADS, HEAD_DIM)
    q = rms_norm(q.reshape(hs), q_norm_g)
    k = rms_norm(k.reshape(hs), k_norm_g)
    v = v.reshape(hs)
    u = a * jax.nn.sigmoid(gt)
    return q, k, v, u


def conv_branch(u_ext, conv_w, conv_b, conv_norm_g):
    c = lax.conv_general_dilated(u_ext, conv_w[:, None, :], (1,), 'VALID',
                                 dimension_numbers=('NWC', 'WIO', 'NWC'),
                                 feature_group_count=CONV_CHANNELS) + conv_b
    return jax.nn.silu(rms_norm(c, conv_norm_g))


def layer_back(x, attn, conv, w_out, norm2_g, w_query, keys1, keys2, expert_u, expert_v):
    mix = jnp.concatenate([attn.reshape(x.shape[:-1] + (ATTN_WIDTH,)), conv], axis=-1)
    h = x + mix @ w_out
    return h + peer_ffn(rms_norm(h, norm2_g), w_query, keys1, keys2, expert_u, expert_v)


def setup_inputs(seed: int = 0) -> dict:
    key = jax.random.key(seed)
    ks = jax.random.split(key, 24)
    f32 = jnp.float32
    n_pages = PAST_LEN // PAGE_SIZE
    n_used = DEC_BATCH * n_pages
    n_phys = n_used + max(1, n_used // 4)
    page_table = jax.random.permutation(ks[0], n_phys)[:n_used].reshape(DEC_BATCH, n_pages).astype(jnp.int32)

    def nrm(k, shape, s):
        return jax.random.normal(k, shape, f32) * s

    return {
        'x_prompt': nrm(ks[1], (BATCH, SEQ, D_MODEL), 1.0),
        'x_sample': nrm(ks[2], (DEC_BATCH, DEC_SEQ, D_MODEL), 1.0),
        'cache_k': nrm(ks[3], (DEPTH, n_phys, PAGE_SIZE, ATTN_HEADS, HEAD_DIM), 1.0),
        'cache_v': nrm(ks[4], (DEPTH, n_phys, PAGE_SIZE, ATTN_HEADS, HEAD_DIM), 1.0),
        'state_conv': nrm(ks[5], (DEPTH, DEC_BATCH, CONV_K - 1, CONV_CHANNELS), 0.5),
        'page_table': page_table,
        'norm1_g': 1.0 + nrm(ks[6], (DEPTH, D_MODEL), 0.1),
        'w_in': nrm(ks[7], (DEPTH, D_MODEL, IN_WIDTH), D_MODEL ** -0.5),
        'q_norm_g': 1.0 + nrm(ks[8], (DEPTH, ATTN_HEADS, HEAD_DIM), 0.1),
        'k_norm_g': 1.0 + nrm(ks[9], (DEPTH, ATTN_HEADS, HEAD_DIM), 0.1),
        'conv_w': nrm(ks[10], (DEPTH, CONV_K, CONV_CHANNELS), CONV_K ** -0.5),
        'conv_b': nrm(ks[11], (DEPTH, CONV_CHANNELS), 0.02),
        'conv_norm_g': 1.0 + nrm(ks[12], (DEPTH, CONV_CHANNELS), 0.1),
        'w_out': nrm(ks[13], (DEPTH, MIX_WIDTH, D_MODEL), MIX_WIDTH ** -0.5),
        'norm2_g': 1.0 + nrm(ks[14], (DEPTH, D_MODEL), 0.1),
        'peer_w_query': nrm(ks[15], (DEPTH, D_MODEL, PEER_HEADS * D_KEY), D_MODEL ** -0.5),
        'peer_keys1': nrm(ks[16], (DEPTH, PEER_HEADS, N_KEYS, HALF_KEY), HALF_KEY ** -0.5),
        'peer_keys2': nrm(ks[17], (DEPTH, PEER_HEADS, N_KEYS, HALF_KEY), HALF_KEY ** -0.5),
        'peer_u': nrm(ks[18], (DEPTH, N_EXPERTS, D_MODEL), D_MODEL ** -0.5),
        'peer_v': nrm(ks[19], (DEPTH, N_EXPERTS, D_MODEL), PEER_HEADS ** -0.5),
    }


def reference(x_prompt, x_sample, cache_k, cache_v, state_conv, page_table,
              norm1_g, w_in, q_norm_g, k_norm_g, conv_w, conv_b, conv_norm_g,
              w_out, norm2_g, peer_w_query, peer_keys1, peer_keys2, peer_u, peer_v):
    xp = x_prompt
    xs = x_sample
    kp_rows, vp_rows, cp_rows, ks_rows, vs_rows, cs_rows = [], [], [], [], [], []
    for l in range(DEPTH):
        peer_args = (peer_w_query[l], peer_keys1[l], peer_keys2[l], peer_u[l], peer_v[l])
        q, k, v, u = layer_front(xp, norm1_g[l], w_in[l], q_norm_g[l], k_norm_g[l])
        attn = moba_prompt(q, k, v)
        conv = conv_branch(jnp.pad(u, ((0, 0), (CONV_K - 1, 0), (0, 0))), conv_w[l], conv_b[l], conv_norm_g[l])
        xp = layer_back(xp, attn, conv, w_out[l], norm2_g[l], *peer_args)
        kp_rows.append(k)
        vp_rows.append(v)
        cp_rows.append(u[:, -(CONV_K - 1):])
        q, k, v, u = layer_front(xs, norm1_g[l], w_in[l], q_norm_g[l], k_norm_g[l])
        attn = moba_sample(q, k, v, cache_k[l], cache_v[l], page_table)
        u_ext = jnp.concatenate([state_conv[l], u], axis=1)
        conv = conv_branch(u_ext, conv_w[l], conv_b[l], conv_norm_g[l])
        xs = layer_back(xs, attn, conv, w_out[l], norm2_g[l], *peer_args)
        ks_rows.append(k)
        vs_rows.append(v)
        cs_rows.append(u_ext[:, -(CONV_K - 1):])
    return (xp, xs, jnp.stack(kp_rows), jnp.stack(vp_rows), jnp.stack(cp_rows),
            jnp.stack(ks_rows), jnp.stack(vs_rows), jnp.stack(cs_rows))
```

```python
import functools

import jax
import jax.numpy as jnp
from jax import lax
from jax.experimental import pallas as pl
from jax.experimental.pallas import tpu as pltpu

F32 = jnp.float32
BF16 = jnp.bfloat16

HEADS = 8
HEAD_DIM = 64
ATTN_W = HEADS * HEAD_DIM
CONV_K = 31
MOBA_BLOCK = 256
MOBA_TOPK = 3
PAGE = 128
PEER_HEADS = 8
N_KEYS = 128
PEER_TOPK = 16
HALF_KEY = 128
EPS = 1e-6
NEG = float("-inf")

LANES = 128
VMEM_LIMIT = 56 * 1024 * 1024


def _cparams(sem):
    return pltpu.CompilerParams(dimension_semantics=sem, vmem_limit_bytes=VMEM_LIMIT)


def _split_bf16(x):
    hi = x.astype(BF16)
    lo = (x - hi.astype(F32)).astype(BF16)
    return hi, lo


def _dot_nt(a, b):
    return lax.dot_general(a, b, (((1,), (1,)), ((), ())), preferred_element_type=F32)


def _dot_nt3(a, b):
    ah, al = _split_bf16(a)
    bh, bl = _split_bf16(b)
    return _dot_nt(ah, bh) + (_dot_nt(ah, bl) + _dot_nt(al, bh))


def _head_blockdiag():
    r = lax.broadcasted_iota(jnp.int32, (ATTN_W, ATTN_W), 0) // HEAD_DIM
    c = lax.broadcasted_iota(jnp.int32, (ATTN_W, ATTN_W), 1) // HEAD_DIM
    return jnp.where(r == c, 1.0 / HEAD_DIM, 0.0).astype(BF16)


def _front_body(x_ref, g1_ref, w_ref, qg_ref, kg_ref, bd_ref,
                q_ref, k_ref, v_ref, u_ref, kb_ref, vb_ref, km_ref, *, n_sub):
    x = x_ref[...]
    ms = jnp.mean(x * x, axis=-1, keepdims=True)
    xn = (x * lax.rsqrt(ms + EPS)) * g1_ref[...]
    proj = jnp.dot(xn.astype(BF16), w_ref[...], preferred_element_type=F32)
    bd = bd_ref[...]

    def head_norm(t, g):
        hi, lo = _split_bf16(t * t)
        msq = jnp.dot(hi, bd, preferred_element_type=F32) + jnp.dot(lo, bd, preferred_element_type=F32)
        return (t * lax.rsqrt(msq + EPS)) * g

    q_ref[...] = head_norm(proj[:, :ATTN_W], qg_ref[...])
    k = head_norm(proj[:, ATTN_W:2 * ATTN_W], kg_ref[...])
    v = proj[:, 2 * ATTN_W:3 * ATTN_W]
    k_ref[...] = k
    v_ref[...] = v
    kb_ref[...] = k.astype(BF16)
    vb_ref[...] = v.astype(BF16)
    a = proj[:, 3 * ATTN_W:3 * ATTN_W + ATTN_W]
    gt = proj[:, 4 * ATTN_W:]
    u_ref[...] = a * jax.nn.sigmoid(gt)
    rows = k.shape[0] // n_sub
    for s in range(n_sub):
        km_ref[0, s:s + 1, :] = jnp.sum(k[s * rows:(s + 1) * rows], axis=0, keepdims=True) * (1.0 / rows)


def _front(x2d, g1, w_in_b, qg, kg, bd, tm, n_sub):
    m, d = x2d.shape
    nw = w_in_b.shape[1]
    row = lambda i: (i, 0)
    fixed = lambda i: (0, 0)
    outs = (
        jax.ShapeDtypeStruct((m, ATTN_W), F32),
        jax.ShapeDtypeStruct((m, ATTN_W), F32),
        jax.ShapeDtypeStruct((m, ATTN_W), F32),
        jax.ShapeDtypeStruct((m, ATTN_W), F32),
        jax.ShapeDtypeStruct((m, ATTN_W), BF16),
        jax.ShapeDtypeStruct((m, ATTN_W), BF16),
        jax.ShapeDtypeStruct((m // tm, n_sub, ATTN_W), F32),
    )
    blk = pl.BlockSpec((tm, ATTN_W), row)
    return pl.pallas_call(
        functools.partial(_front_body, n_sub=n_sub),
        grid=(m // tm,),
        in_specs=[pl.BlockSpec((tm, d), row), pl.BlockSpec((1, d), fixed),
                  pl.BlockSpec((d, nw), fixed), pl.BlockSpec((1, ATTN_W), fixed),
                  pl.BlockSpec((1, ATTN_W), fixed), pl.BlockSpec((ATTN_W, ATTN_W), fixed)],
        out_specs=(blk, blk, blk, blk, blk, blk,
                   pl.BlockSpec((1, n_sub, ATTN_W), lambda i: (i, 0, 0))),
        out_shape=outs,
        compiler_params=_cparams(("arbitrary",)),
        name="front",
    )(x2d, g1, w_in_b, qg, kg, bd)


HIST = 32


def _conv_body(u_ref, h0_ref, w_ref, b_ref, g_ref, o_ref, buf_ref, *, ts, chunk):
    t = pl.program_id(1)

    @pl.when(t == 0)
    def _():
        buf_ref[0:HIST, :] = h0_ref[0]

    @pl.when(t != 0)
    def _():
        buf_ref[0:HIST, :] = buf_ref[ts:ts + HIST, :]

    buf_ref[HIST:HIST + ts, :] = u_ref[0]
    w = w_ref[...]
    off = HIST - (CONV_K - 1)
    for c in range(ts // chunk):
        acc = jnp.zeros((chunk, u_ref.shape[2]), F32)
        for j in range(CONV_K):
            acc = acc + buf_ref[c * chunk + off + j:c * chunk + off + j + chunk, :] * w[j:j + 1, :]
        y = acc + b_ref[...]
        ms = jnp.mean(y * y, axis=-1, keepdims=True)
        y = (y * lax.rsqrt(ms + EPS)) * g_ref[...]
        o_ref[0, c * chunk:(c + 1) * chunk, :] = y * jax.nn.sigmoid(y)


def _conv(u3, hist0, conv_w, conv_b, conv_g, ts, chunk):
    b, l, c = u3.shape
    fixed = lambda i, t: (0, 0)
    return pl.pallas_call(
        functools.partial(_conv_body, ts=ts, chunk=chunk),
        grid=(b, l // ts),
        in_specs=[pl.BlockSpec((1, ts, c), lambda i, t: (i, t, 0)),
                  pl.BlockSpec((1, HIST, c), lambda i, t: (i, 0, 0)),
                  pl.BlockSpec((CONV_K, c), fixed), pl.BlockSpec((1, c), fixed),
                  pl.BlockSpec((1, c), fixed)],
        out_specs=pl.BlockSpec((1, ts, c), lambda i, t: (i, t, 0)),
        out_shape=jax.ShapeDtypeStruct((b, l, c), F32),
        scratch_shapes=[pltpu.VMEM((ts + HIST, c), F32)],
        compiler_params=_cparams(("arbitrary", "arbitrary")),
        name="conv_branch",
    )(u3, hist0, conv_w, conv_b, conv_g)


def _block_select(gate, lane_blk, n_valid, nb):
    width = gate.shape[1]
    valid = lane_blk < n_valid
    g = jnp.where(valid, gate, NEG)
    rank = jnp.zeros(gate.shape, jnp.int32)
    for s in range(1, nb):
        fwd = pltpu.roll(g, width - s, 1)
        bwd = pltpu.roll(g, nb - s, 1)
        wraps = lane_blk + s >= nb
        other = jnp.where(wraps, bwd, fwd)
        other_first = wraps
        beats = (other > g) | ((other == g) & other_first)
        rank = rank + beats.astype(jnp.int32)
    return valid & (rank < MOBA_TOPK)


def _attn_body(q_ref, kb_ref, vb_ref, km_ref, o_ref, *, nb):
    i = pl.program_id(1)
    blk = MOBA_BLOCK
    q = q_ref[0]
    width = HEADS * nb
    km = km_ref[0]
    kmt = jnp.concatenate([km] * HEADS, axis=0)
    r_head = lax.broadcasted_iota(jnp.int32, kmt.shape, 0) // nb
    c_head = lax.broadcasted_iota(jnp.int32, kmt.shape, 1) // HEAD_DIM
    kmt = jnp.where(r_head == c_head, kmt, 0.0)
    gate = _dot_nt3(q, kmt)
    lane = lax.broadcasted_iota(jnp.int32, (blk, width), 1)
    sel = _block_select(gate, lane % nb, i, nb).astype(F32)

    row = lax.broadcasted_iota(jnp.int32, (blk, blk), 0)
    col = lax.broadcasted_iota(jnp.int32, (blk, blk), 1)
    causal = col <= row
    lane_s = lax.broadcasted_iota(jnp.int32, (blk, LANES), 1)
    scale = HEAD_DIM ** -0.5
    own0 = pl.multiple_of(i * blk, blk)

    for c in range(ATTN_W // LANES):
        slab = slice(c * LANES, (c + 1) * LANES)
        qs = q[:, slab] * scale
        outs = []
        for half in range(LANES // HEAD_DIM):
            h = c * (LANES // HEAD_DIM) + half
            in_head = (lane_s // HEAD_DIM) == half
            qm = jnp.where(in_head, qs, 0.0).astype(BF16)
            k_own = kb_ref[0, pl.ds(own0, blk), slab]
            v_own = vb_ref[0, pl.ds(own0, blk), slab]
            s = jnp.where(causal, _dot_nt(qm, k_own), NEG)
            m0 = jnp.max(s, axis=-1, keepdims=True)
            p = jnp.exp(s - m0)
            l0 = jnp.sum(p, axis=-1, keepdims=True)
            a0 = jnp.dot(p.astype(BF16), v_own, preferred_element_type=F32)

            def past(j, carry, h=h, qm=qm, slab=slab):
                m, l, acc = carry
                j0 = pl.multiple_of(j * blk, blk)
                kj = kb_ref[0, pl.ds(j0, blk), slab]
                vj = vb_ref[0, pl.ds(j0, blk), slab]
                picked = jnp.max(jnp.where(lane == h * nb + j, sel, 0.0), axis=-1, keepdims=True)
                s = jnp.where(picked > 0.0, _dot_nt(qm, kj), NEG)
                m_new = jnp.maximum(m, jnp.max(s, axis=-1, keepdims=True))
                alpha = jnp.exp(m - m_new)
                p = jnp.exp(s - m_new)
                l = alpha * l + jnp.sum(p, axis=-1, keepdims=True)
                acc = alpha * acc + jnp.dot(p.astype(BF16), vj, preferred_element_type=F32)
                return m_new, l, acc

            m, l, acc = lax.fori_loop(0, i, past, (m0, l0, a0))
            outs.append(acc / l)
        o_ref[0, :, slab] = jnp.where((lane_s // HEAD_DIM) == 0, outs[0], outs[1])


def _moba_prompt(q3, kb3, vb3, km3):
    b, s, w = q3.shape
    nb = s // MOBA_BLOCK
    return pl.pallas_call(
        functools.partial(_attn_body, nb=nb),
        grid=(b, nb),
        in_specs=[pl.BlockSpec((1, MOBA_BLOCK, w), lambda bi, i: (bi, i, 0)),
                  pl.BlockSpec((1, s, w), lambda bi, i: (bi, 0, 0)),
                  pl.BlockSpec((1, s, w), lambda bi, i: (bi, 0, 0)),
                  pl.BlockSpec((1, nb, w), lambda bi, i: (bi, 0, 0))],
        out_specs=pl.BlockSpec((1, MOBA_BLOCK, w), lambda bi, i: (bi, i, 0)),
        out_shape=jax.ShapeDtypeStruct((b, s, w), F32),
        compiler_params=_cparams(("arbitrary", "arbitrary")),
        name="moba_prompt",
    )(q3, kb3, vb3, km3)


def _back_body(x_ref, a_ref, c_ref, w_ref, g2_ref, h_ref, hn_ref):
    mix = jnp.concatenate([a_ref[...], c_ref[...]], axis=-1).astype(BF16)
    h = x_ref[...] + jnp.dot(mix, w_ref[...], preferred_element_type=F32)
    h_ref[...] = h
    ms = jnp.mean(h * h, axis=-1, keepdims=True)
    hn_ref[...] = ((h * lax.rsqrt(ms + EPS)) * g2_ref[...]).astype(BF16)


def _back(x2d, attn2d, conv2d, w_out_b, g2, tm):
    m, d = x2d.shape
    row = lambda i: (i, 0)
    fixed = lambda i: (0, 0)
    return pl.pallas_call(
        _back_body,
        grid=(m // tm,),
        in_specs=[pl.BlockSpec((tm, d), row), pl.BlockSpec((tm, ATTN_W), row),
                  pl.BlockSpec((tm, d - ATTN_W), row), pl.BlockSpec((d, d), fixed),
                  pl.BlockSpec((1, d), fixed)],
        out_specs=(pl.BlockSpec((tm, d), row), pl.BlockSpec((tm, d), row)),
        out_shape=(jax.ShapeDtypeStruct((m, d), F32), jax.ShapeDtypeStruct((m, d), BF16)),
        compiler_params=_cparams(("arbitrary",)),
        name="back",
    )(x2d, attn2d, conv2d, w_out_b, g2)


def _pair_candidates():
    return [(r1, r2) for r1 in range(PEER_TOPK) for r2 in range(PEER_TOPK)
            if (r1 + 1) * (r2 + 1) <= PEER_TOPK]


def _top_rounds(s, k):
    n = s.shape[0]
    idx = lax.broadcasted_iota(jnp.int32, s.shape, 0)
    rank = jnp.full(s.shape, k, jnp.int32)
    vals = []
    for r in range(k):
        m = jnp.max(s, axis=0, keepdims=True)
        first = jnp.min(jnp.where(s == m, idx, n), axis=0, keepdims=True)
        hit = idx == first
        rank = jnp.where(hit, r, rank)
        s = jnp.where(hit, NEG, s)
        vals.append(m)
    return vals, rank


def _route_body(hn_ref, wq_ref, k1_ref, k2_ref, e1w_ref, cnt_ref, e2_ref, rk2_ref, qh_ref):
    tm = hn_ref.shape[0]
    qh_ref[...] = _dot_nt(wq_ref[...], hn_ref[...])
    pairs = _pair_candidates()
    n_pad = (-len(pairs)) % 8

    def head(h, carry):
        base = pl.multiple_of(h * (2 * HALF_KEY), 2 * HALF_KEY)
        q1h, q1l = _split_bf16(qh_ref[pl.ds(base, HALF_KEY), :])
        q2h, q2l = _split_bf16(qh_ref[pl.ds(base + HALF_KEY, HALF_KEY), :])
        k1h, k1l = _split_bf16(k1_ref[h])
        k2h, k2l = _split_bf16(k2_ref[h])
        dot = functools.partial(jnp.dot, preferred_element_type=F32)
        s1 = dot(k1h, q1h) + (dot(k1h, q1l) + dot(k1l, q1h))
        s2 = dot(k2h, q2h) + (dot(k2h, q2l) + dot(k2l, q2h))
        v1, rk1 = _top_rounds(s1, PEER_TOPK)
        v2, rk2 = _top_rounds(s2, PEER_TOPK)
        cand = jnp.concatenate([v1[r1] + v2[r2] for r1, r2 in pairs]
                               + [jnp.full((n_pad, tm), NEG, F32)], axis=0)
        _, rkc = _top_rounds(cand, PEER_TOPK)
        picked = (rkc < PEER_TOPK).astype(F32)
        ez = jnp.exp(cand - cand[0:1, :]) * picked
        z = jnp.sum(ez, axis=0, keepdims=True)
        cnt = jnp.zeros((N_KEYS, tm), F32)
        for r1 in range(PEER_TOPK):
            rows = [i for i, (a, _) in enumerate(pairs) if a == r1]
            c_r1 = jnp.sum(picked[rows[0]:rows[-1] + 1, :], axis=0, keepdims=True)
            cnt = jnp.where(rk1 == r1, c_r1, cnt)
        e1w_ref[h] = jnp.exp(s1 - v1[0]) / z
        cnt_ref[h] = cnt
        e2_ref[h] = jnp.exp(s2 - v2[0])
        rk2_ref[h] = rk2.astype(F32)
        return carry

    lax.fori_loop(0, PEER_HEADS, head, 0)


def _route(hn, wq_t, keys1, keys2, tm):
    m, d = hn.shape
    nq = wq_t.shape[0]
    out = jax.ShapeDtypeStruct((PEER_HEADS, N_KEYS, m), F32)
    oblk = pl.BlockSpec((PEER_HEADS, N_KEYS, tm), lambda i: (0, 0, i))
    kblk = pl.BlockSpec((PEER_HEADS, N_KEYS, HALF_KEY), lambda i: (0, 0, 0))
    return pl.pallas_call(
        _route_body,
        grid=(m // tm,),
        in_specs=[pl.BlockSpec((tm, d), lambda i: (i, 0)), pl.BlockSpec((nq, d), lambda i: (0, 0)),
                  kblk, kblk],
        out_specs=(oblk, oblk, oblk, oblk),
        out_shape=(out, out, out, out),
        scratch_shapes=[pltpu.VMEM((nq, tm), F32)],
        compiler_params=_cparams(("arbitrary",)),
        name="peer_route",
    )(hn, wq_t, keys1, keys2)


def _experts_body(hn_ref, u_ref, v_ref, e1w_ref, cnt_ref, e2_ref, rk2_ref, h_ref, y_ref, acc_ref, *, n_a):
    e = pl.program_id(1)

    @pl.when(e == 0)
    def _():
        acc_ref[...] = jnp.zeros_like(acc_ref)

    pre = _dot_nt(u_ref[...], hn_ref[...])
    act = pre * 0.5 * (1.0 + lax.erf(pre * (2.0 ** -0.5)))
    parts = []
    for al in range(n_a):
        a = e * n_a + al
        g = jnp.zeros((N_KEYS, hn_ref.shape[0]), F32)
        for h in range(PEER_HEADS):
            c_row = cnt_ref[h, pl.ds(a, 1), :]
            w_row = e1w_ref[h, pl.ds(a, 1), :]
            g = g + jnp.where(rk2_ref[h] < c_row, e2_ref[h], 0.0) * w_row
        parts.append((g * act[al * N_KEYS:(al + 1) * N_KEYS]).astype(BF16))
    p = jnp.concatenate(parts, axis=0)
    acc_ref[...] += lax.dot_general(p, v_ref[...], (((0,), (0,)), ((), ())),
                                    preferred_element_type=F32)

    @pl.when(e == pl.num_programs(1) - 1)
    def _():
        y_ref[...] = h_ref[...] + acc_ref[...]


def _experts(hn, u_b, v_b, e1w, cnt, e2, rk2, h, tm, te):
    m, d = hn.shape
    ne = u_b.shape[0]
    rblk = pl.BlockSpec((PEER_HEADS, N_KEYS, tm), lambda i, e: (0, 0, i))
    return pl.pallas_call(
        functools.partial(_experts_body, n_a=te // N_KEYS),
        grid=(m // tm, ne // te),
        in_specs=[pl.BlockSpec((tm, d), lambda i, e: (i, 0)),
                  pl.BlockSpec((te, d), lambda i, e: (e, 0)),
                  pl.BlockSpec((te, d), lambda i, e: (e, 0)),
                  rblk, rblk, rblk, rblk,
                  pl.BlockSpec((tm, d), lambda i, e: (i, 0))],
        out_specs=pl.BlockSpec((tm, d), lambda i, e: (i, 0)),
        out_shape=jax.ShapeDtypeStruct((m, d), F32),
        scratch_shapes=[pltpu.VMEM((tm, d), F32)],
        compiler_params=_cparams(("arbitrary", "arbitrary")),
        name="peer_experts",
    )(hn, u_b, v_b, e1w, cnt, e2, rk2, h)


def _sample_attention_xla(q, k_new, v_new, k_pool, v_pool, page_table):
    db, t, h, dh = q.shape
    n_pages = page_table.shape[1]
    ppb = MOBA_BLOCK // PAGE
    nb_past = (n_pages * PAGE) // MOBA_BLOCK
    scale = dh ** -0.5
    page_mean = jnp.mean(k_pool, axis=1, dtype=F32)
    pm_seq = page_mean[page_table[:, :nb_past * ppb]]
    k_mean = pm_seq.reshape(db, nb_past, ppb, h, dh).mean(axis=2)
    gate = jnp.einsum('bthd,bnhd->bthn', q, k_mean, precision=lax.Precision.HIGHEST)
    _, sel = lax.top_k(gate, MOBA_TOPK)
    logical = sel[..., None] * ppb + jnp.arange(ppb)
    phys = page_table[jnp.arange(db)[:, None, None, None, None], logical]
    h_idx = jnp.arange(h)[None, None, :, None, None]
    k_sel = k_pool[phys, :, h_idx].reshape(db, t, h, MOBA_TOPK * MOBA_BLOCK, dh)
    v_sel = v_pool[phys, :, h_idx].reshape(db, t, h, MOBA_TOPK * MOBA_BLOCK, dh)
    s_loc = jnp.einsum('bthd,blhd->bthl', q, k_new) * scale
    s_loc = jnp.where(jnp.tril(jnp.ones((t, t), bool))[None, :, None, :], s_loc, NEG)
    s_sel = jnp.einsum('bthd,bthnd->bthn', q, k_sel) * scale
    p = jax.nn.softmax(jnp.concatenate([s_sel, s_loc], axis=-1), axis=-1)
    n_s = s_sel.shape[-1]
    return (jnp.einsum('bthn,bthnd->bthd', p[..., :n_s], v_sel)
            + jnp.einsum('bthl,blhd->bthd', p[..., n_s:], v_new))


def _mixer_tail(x2d, attn2d, conv2d, w_out_b, g2, wq_t, keys1, keys2, u_b, v_b, tm, te):
    h, hn = _back(x2d, attn2d, conv2d, w_out_b, g2, tm)
    routing = _route(hn, wq_t, keys1, keys2, tm)
    return _experts(hn, u_b, v_b, *routing, h, tm, te)


TILE_M = 512
TILE_PEER = 256
TILE_E = 512
CONV_CHUNK = 64


def kernel(x_prompt, x_sample, cache_k, cache_v, state_conv, page_table, norm1_g, w_in, q_norm_g, k_norm_g, conv_w, conv_b, conv_norm_g, w_out, norm2_g, peer_w_query, peer_keys1, peer_keys2, peer_u, peer_v):
    depth = w_in.shape[0]
    bsz, seq, d = x_prompt.shape
    db, dt, _ = x_sample.shape
    assert seq % TILE_M == 0 and TILE_M % MOBA_BLOCK == 0 and d - ATTN_W == ATTN_W
    assert (page_table.shape[1] * PAGE) % MOBA_BLOCK == 0, "partially filled own block is not supported"
    bd = _head_blockdiag()
    xp, xs = x_prompt, x_sample
    rows = {n: [] for n in ("kp", "vp", "cp", "ks", "vs", "cs")}
    for l in range(depth):
        w_in_b = w_in[l].astype(BF16)
        w_out_b = w_out[l].astype(BF16)
        wq_t = peer_w_query[l].T.astype(BF16)
        u_b = peer_u[l].astype(BF16)
        v_b = peer_v[l].astype(BF16)
        g1 = norm1_g[l][None]
        g2 = norm2_g[l][None]
        qg = q_norm_g[l].reshape(1, ATTN_W)
        kg = k_norm_g[l].reshape(1, ATTN_W)
        cb = conv_b[l][None]
        cg = conv_norm_g[l][None]
        tail = functools.partial(_mixer_tail, w_out_b=w_out_b, g2=g2, wq_t=wq_t, keys1=peer_keys1[l],
                                 keys2=peer_keys2[l], u_b=u_b, v_b=v_b, te=TILE_E)

        x2d = xp.reshape(bsz * seq, d)
        q, k, v, u, kb, vb, km = _front(x2d, g1, w_in_b, qg, kg, bd, TILE_M, TILE_M // MOBA_BLOCK)
        r3 = lambda t: t.reshape(bsz, seq, ATTN_W)
        attn = _moba_prompt(r3(q), r3(kb), r3(vb), km.reshape(bsz, seq // MOBA_BLOCK, ATTN_W))
        conv = _conv(r3(u), jnp.zeros((bsz, HIST, ATTN_W), F32), conv_w[l], cb, cg, TILE_M, CONV_CHUNK)
        y = tail(x2d, attn.reshape(-1, ATTN_W), conv.reshape(-1, ATTN_W), tm=TILE_PEER)
        xp = y.reshape(bsz, seq, d)
        rows["kp"].append(k.reshape(bsz, seq, HEADS, HEAD_DIM))
        rows["vp"].append(v.reshape(bsz, seq, HEADS, HEAD_DIM))
        rows["cp"].append(r3(u)[:, -(CONV_K - 1):])

        n_s = db * dt
        x2d = xs.reshape(n_s, d)
        q, k, v, u, _, _, _ = _front(x2d, g1, w_in_b, qg, kg, bd, n_s, 1)
        h4 = lambda t: t.reshape(db, dt, HEADS, HEAD_DIM)
        attn = _sample_attention_xla(h4(q), h4(k), h4(v), cache_k[l], cache_v[l], page_table)
        u3 = u.reshape(db, dt, ATTN_W)
        hist = jnp.concatenate([jnp.zeros((db, HIST - (CONV_K - 1), ATTN_W), F32), state_conv[l]], axis=1)
        conv = _conv(u3, hist, conv_w[l], cb, cg, dt, dt)
        y = tail(x2d, attn.reshape(n_s, ATTN_W), conv.reshape(n_s, ATTN_W), tm=n_s)
        xs = y.reshape(db, dt, d)
        rows["ks"].append(h4(k))
        rows["vs"].append(h4(v))
        rows["cs"].append(jnp.concatenate([state_conv[l], u3], axis=1)[:, -(CONV_K - 1):])
    return (xp, xs, jnp.stack(rows["kp"]), jnp.stack(rows["vp"]), jnp.stack(rows["cp"]),
            jnp.stack(rows["ks"]), jnp.stack(rows["vs"]), jnp.stack(rows["cs"]))
```

```python
import functools

import jax
import jax.numpy as jnp
from jax import lax
from jax.experimental import pallas as pl
from jax.experimental.pallas import tpu as pltpu

F32 = jnp.float32
BF16 = jnp.bfloat16

HEADS = 8
HEAD_DIM = 64
ATTN_W = HEADS * HEAD_DIM
CONV_K = 31
MOBA_BLOCK = 256
MOBA_TOPK = 3
PAGE = 128
PEER_HEADS = 8
N_KEYS = 128
PEER_TOPK = 16
HALF_KEY = 128
EPS = 1e-6
NEG = float("-inf")

LANES = 128
BF16_ROWS = 16
EXPERT_CHUNK_KEYS = 8
VMEM_LIMIT = 56 * 1024 * 1024


def _cparams(sem):
    return pltpu.CompilerParams(dimension_semantics=sem, vmem_limit_bytes=VMEM_LIMIT)


def _split_bf16(x):
    hi = x.astype(BF16)
    lo = (x - hi.astype(F32)).astype(BF16)
    return hi, lo


def _dot_nt(a, b):
    return lax.dot_general(a, b, (((1,), (1,)), ((), ())), preferred_element_type=F32)


def _dot_nt3(a, b):
    ah, al = _split_bf16(a)
    bh, bl = _split_bf16(b)
    return _dot_nt(ah, bh) + (_dot_nt(ah, bl) + _dot_nt(al, bh))


def _head_blockdiag():
    r = lax.broadcasted_iota(jnp.int32, (ATTN_W, ATTN_W), 0) // HEAD_DIM
    c = lax.broadcasted_iota(jnp.int32, (ATTN_W, ATTN_W), 1) // HEAD_DIM
    return jnp.where(r == c, 1.0 / HEAD_DIM, 0.0).astype(BF16)


def _front_body(x_ref, g1_ref, w_ref, qg_ref, kg_ref, bd_ref,
                q_ref, k_ref, v_ref, u_ref, kb_ref, vb_ref, km_ref, *, n_sub):
    x = x_ref[...]
    ms = jnp.mean(x * x, axis=-1, keepdims=True)
    xn = (x * lax.rsqrt(ms + EPS)) * g1_ref[...]
    proj = jnp.dot(xn.astype(BF16), w_ref[...], preferred_element_type=F32)
    bd = bd_ref[...]

    def head_norm(t, g):
        hi, lo = _split_bf16(t * t)
        msq = jnp.dot(hi, bd, preferred_element_type=F32) + jnp.dot(lo, bd, preferred_element_type=F32)
        return (t * lax.rsqrt(msq + EPS)) * g

    q_ref[...] = head_norm(proj[:, :ATTN_W], qg_ref[...])
    k = head_norm(proj[:, ATTN_W:2 * ATTN_W], kg_ref[...])
    v = proj[:, 2 * ATTN_W:3 * ATTN_W]
    k_ref[...] = k
    v_ref[...] = v
    kb_ref[...] = k.astype(BF16)
    vb_ref[...] = v.astype(BF16)
    a = proj[:, 3 * ATTN_W:3 * ATTN_W + ATTN_W]
    gt = proj[:, 4 * ATTN_W:]
    u_ref[...] = a * jax.nn.sigmoid(gt)
    rows = k.shape[0] // n_sub
    for s in range(n_sub):
        km_ref[0, s:s + 1, :] = jnp.sum(k[s * rows:(s + 1) * rows], axis=0, keepdims=True) * (1.0 / rows)


def _front(x2d, g1, w_in_b, qg, kg, bd, tm, n_sub):
    m, d = x2d.shape
    nw = w_in_b.shape[1]
    row = lambda i: (i, 0)
    fixed = lambda i: (0, 0)
    outs = (
        jax.ShapeDtypeStruct((m, ATTN_W), F32),
        jax.ShapeDtypeStruct((m, ATTN_W), F32),
        jax.ShapeDtypeStruct((m, ATTN_W), F32),
        jax.ShapeDtypeStruct((m, ATTN_W), F32),
        jax.ShapeDtypeStruct((m, ATTN_W), BF16),
        jax.ShapeDtypeStruct((m, ATTN_W), BF16),
        jax.ShapeDtypeStruct((m // tm, n_sub, ATTN_W), F32),
    )
    blk = pl.BlockSpec((tm, ATTN_W), row)
    return pl.pallas_call(
        functools.partial(_front_body, n_sub=n_sub),
        grid=(m // tm,),
        in_specs=[pl.BlockSpec((tm, d), row), pl.BlockSpec((1, d), fixed),
                  pl.BlockSpec((d, nw), fixed), pl.BlockSpec((1, ATTN_W), fixed),
                  pl.BlockSpec((1, ATTN_W), fixed), pl.BlockSpec((ATTN_W, ATTN_W), fixed)],
        out_specs=(blk, blk, blk, blk, blk, blk,
                   pl.BlockSpec((1, n_sub, ATTN_W), lambda i: (i, 0, 0))),
        out_shape=outs,
        compiler_params=_cparams(("arbitrary",)),
        name="front",
    )(x2d, g1, w_in_b, qg, kg, bd)


HIST = 32


def _conv_body(u_ref, h0_ref, w_ref, b_ref, g_ref, o_ref, buf_ref, *, ts, chunk):
    t = pl.program_id(1)

    @pl.when(t == 0)
    def _():
        buf_ref[0:HIST, :] = h0_ref[0]

    @pl.when(t != 0)
    def _():
        buf_ref[0:HIST, :] = buf_ref[ts:ts + HIST, :]

    buf_ref[HIST:HIST + ts, :] = u_ref[0]
    w = w_ref[...]
    off = HIST - (CONV_K - 1)
    for c in range(ts // chunk):
        acc = jnp.zeros((chunk, u_ref.shape[2]), F32)
        for j in range(CONV_K):
            acc = acc + buf_ref[c * chunk + off + j:c * chunk + off + j + chunk, :] * w[j:j + 1, :]
        y = acc + b_ref[...]
        ms = jnp.mean(y * y, axis=-1, keepdims=True)
        y = (y * lax.rsqrt(ms + EPS)) * g_ref[...]
        o_ref[0, c * chunk:(c + 1) * chunk, :] = y * jax.nn.sigmoid(y)


def _conv(u3, hist0, conv_w, conv_b, conv_g, ts, chunk):
    b, l, c = u3.shape
    fixed = lambda i, t: (0, 0)
    return pl.pallas_call(
        functools.partial(_conv_body, ts=ts, chunk=chunk),
        grid=(b, l // ts),
        in_specs=[pl.BlockSpec((1, ts, c), lambda i, t: (i, t, 0)),
                  pl.BlockSpec((1, HIST, c), lambda i, t: (i, 0, 0)),
                  pl.BlockSpec((CONV_K, c), fixed), pl.BlockSpec((1, c), fixed),
                  pl.BlockSpec((1, c), fixed)],
        out_specs=pl.BlockSpec((1, ts, c), lambda i, t: (i, t, 0)),
        out_shape=jax.ShapeDtypeStruct((b, l, c), F32),
        scratch_shapes=[pltpu.VMEM((ts + HIST, c), F32)],
        compiler_params=_cparams(("arbitrary", "arbitrary")),
        name="conv_branch",
    )(u3, hist0, conv_w, conv_b, conv_g)


def _block_select(gate, blk_idx, n_valid, nb):
    height = gate.shape[0]
    valid = blk_idx < n_valid
    g = jnp.where(valid, gate, NEG)
    rank = jnp.zeros(gate.shape, jnp.int32)
    for s in range(1, nb):
        fwd = pltpu.roll(g, height - s, 0)
        bwd = pltpu.roll(g, nb - s, 0)
        wraps = blk_idx + s >= nb
        other = jnp.where(wraps, bwd, fwd)
        beats = (other > g) | ((other == g) & wraps)
        rank = rank + beats.astype(jnp.int32)
    return valid & (rank < MOBA_TOPK)


MASK_BIAS = -1e30
ATTN_GROUP = 4


def _attn_body(q_ref, kb_ref, vb_ref, km_ref, ind_ref, o_ref, s_ref, mx_ref, l_ref, acc_ref, *, nb):
    i = pl.program_id(1)
    blk = MOBA_BLOCK
    hps = LANES // HEAD_DIM
    q = q_ref[0]
    km = km_ref[0]
    pad_rows = LANES - HEADS * nb
    kmt = jnp.concatenate([km] * HEADS + ([jnp.zeros((pad_rows, ATTN_W), F32)] if pad_rows else []), axis=0)
    r_head = lax.broadcasted_iota(jnp.int32, kmt.shape, 0) // nb
    c_head = lax.broadcasted_iota(jnp.int32, kmt.shape, 1) // HEAD_DIM
    kmt = jnp.where(r_head == c_head, kmt, 0.0)
    gate_t = _dot_nt3(kmt, q)
    rix = lax.broadcasted_iota(jnp.int32, (LANES, blk), 0)
    n_valid = jnp.where(rix < HEADS * nb, i, 0)
    sel_t = _block_select(gate_t, rix % nb, n_valid, nb)
    bias = jnp.where(sel_t, 0.0, MASK_BIAS).T
    lane = lax.broadcasted_iota(jnp.int32, (blk, LANES), 1)

    row = lax.broadcasted_iota(jnp.int32, (hps * blk, blk), 0) % blk
    col = lax.broadcasted_iota(jnp.int32, (hps * blk, blk), 1)
    causal = col <= row
    scale = HEAD_DIM ** -0.5
    own0 = pl.multiple_of(i * blk, blk)
    grp = ATTN_GROUP * blk
    n_grp = (i + ATTN_GROUP) // ATTN_GROUP

    def tilemax(s):
        out = s[:, :LANES]
        for t in range(1, s.shape[1] // LANES):
            out = jnp.maximum(out, s[:, t * LANES:(t + 1) * LANES])
        return out

    def tilesum(s):
        out = s[:, :LANES]
        for t in range(1, s.shape[1] // LANES):
            out = out + s[:, t * LANES:(t + 1) * LANES]
        return out

    for c in range(ATTN_W // LANES):
        slab = slice(c * LANES, (c + 1) * LANES)
        qs = q[:, slab] * scale
        stacked = []
        for half in range(hps):
            h = c * hps + half
            qm = jnp.where((lane // HEAD_DIM) == half, qs, 0.0).astype(BF16)
            bh = jnp.where((lane // nb) == h, bias, 0.0).astype(BF16)
            stacked.append(jnp.concatenate([qm, bh], axis=1))
        q2 = jnp.concatenate(stacked, axis=0)

        mx_ref[...] = jnp.full(mx_ref.shape, NEG, F32)

        def scores(g, carry, q2=q2, slab=slab):
            g0 = pl.multiple_of(g * grp, grp)
            kj = jnp.concatenate([kb_ref[0, pl.ds(g0, grp), slab], ind_ref[pl.ds(g0, grp), :]], axis=1)
            s = _dot_nt(q2, kj)
            s_ref[:, pl.ds(g0, grp)] = s
            mx_ref[...] = jnp.maximum(mx_ref[...], tilemax(s))
            return carry

        lax.fori_loop(0, n_grp, scores, 0)
        s = jnp.where(causal, _dot_nt(q2[:, :LANES], kb_ref[0, pl.ds(own0, blk), slab]), NEG)
        s_ref[:, pl.ds(own0, blk)] = s
        m = jnp.max(jnp.maximum(mx_ref[...], tilemax(s)), axis=-1, keepdims=True)
        l_ref[...] = jnp.zeros_like(l_ref)
        acc_ref[...] = jnp.zeros_like(acc_ref)

        def weigh(g, carry, m=m, slab=slab):
            g0 = pl.multiple_of(g * grp, grp)
            p = jnp.exp(s_ref[:, pl.ds(g0, grp)] - m)
            l_ref[...] += tilesum(p)
            acc_ref[...] += jnp.dot(p.astype(BF16), vb_ref[0, pl.ds(g0, grp), slab],
                                    preferred_element_type=F32)
            return carry

        lax.fori_loop(0, n_grp, weigh, 0)
        o = acc_ref[...] / jnp.sum(l_ref[...], axis=-1, keepdims=True)
        out = o[0:blk]
        for half in range(1, hps):
            out = jnp.where((lane // HEAD_DIM) == half, o[half * blk:(half + 1) * blk], out)
        o_ref[0, :, slab] = out


def _moba_prompt(q3, kb3, vb3, km3):
    b, s, w = q3.shape
    nb = s // MOBA_BLOCK
    assert HEADS * nb <= LANES and nb % ATTN_GROUP == 0
    rows = (LANES // HEAD_DIM) * MOBA_BLOCK
    key_blk = lax.broadcasted_iota(jnp.int32, (s, LANES), 0) // MOBA_BLOCK
    lane = lax.broadcasted_iota(jnp.int32, (s, LANES), 1)
    ind = ((key_blk == lane % nb) & (lane < HEADS * nb)).astype(BF16)
    return pl.pallas_call(
        functools.partial(_attn_body, nb=nb),
        grid=(b, nb),
        in_specs=[pl.BlockSpec((1, MOBA_BLOCK, w), lambda bi, i: (bi, i, 0)),
                  pl.BlockSpec((1, s, w), lambda bi, i: (bi, 0, 0)),
                  pl.BlockSpec((1, s, w), lambda bi, i: (bi, 0, 0)),
                  pl.BlockSpec((1, nb, w), lambda bi, i: (bi, 0, 0)),
                  pl.BlockSpec((s, LANES), lambda bi, i: (0, 0))],
        out_specs=pl.BlockSpec((1, MOBA_BLOCK, w), lambda bi, i: (bi, i, 0)),
        out_shape=jax.ShapeDtypeStruct((b, s, w), F32),
        scratch_shapes=[pltpu.VMEM((rows, s), F32), pltpu.VMEM((rows, LANES), F32),
                        pltpu.VMEM((rows, LANES), F32), pltpu.VMEM((rows, LANES), F32)],
        compiler_params=_cparams(("arbitrary", "arbitrary")),
        name="moba_prompt",
    )(q3, kb3, vb3, km3, ind)


def _back_body(x_ref, a_ref, c_ref, w_ref, g2_ref, h_ref, hn_ref):
    mix = jnp.concatenate([a_ref[...], c_ref[...]], axis=-1).astype(BF16)
    h = x_ref[...] + jnp.dot(mix, w_ref[...], preferred_element_type=F32)
    h_ref[...] = h
    ms = jnp.mean(h * h, axis=-1, keepdims=True)
    hn_ref[...] = ((h * lax.rsqrt(ms + EPS)) * g2_ref[...]).astype(BF16)


def _back(x2d, attn2d, conv2d, w_out_b, g2, tm):
    m, d = x2d.shape
    row = lambda i: (i, 0)
    fixed = lambda i: (0, 0)
    return pl.pallas_call(
        _back_body,
        grid=(m // tm,),
        in_specs=[pl.BlockSpec((tm, d), row), pl.BlockSpec((tm, ATTN_W), row),
                  pl.BlockSpec((tm, d - ATTN_W), row), pl.BlockSpec((d, d), fixed),
                  pl.BlockSpec((1, d), fixed)],
        out_specs=(pl.BlockSpec((tm, d), row), pl.BlockSpec((tm, d), row)),
        out_shape=(jax.ShapeDtypeStruct((m, d), F32), jax.ShapeDtypeStruct((m, d), BF16)),
        compiler_params=_cparams(("arbitrary",)),
        name="back",
    )(x2d, attn2d, conv2d, w_out_b, g2)


def _pair_candidates():
    return [(r1, r2) for r1 in range(PEER_TOPK) for r2 in range(PEER_TOPK)
            if (r1 + 1) * (r2 + 1) <= PEER_TOPK]


def _top_rounds(s, k, exact):
    n = s.shape[0]
    idx = lax.broadcasted_iota(jnp.int32, s.shape, 0)
    rank = jnp.full(s.shape, k, jnp.int32)
    vals = []
    for r in range(k):
        m = jnp.max(s, axis=0, keepdims=True)
        hit = s == m
        if exact:
            first = jnp.min(jnp.where(hit, idx, n), axis=0, keepdims=True)
            hit = idx == first
        rank = jnp.where(hit, r, rank)
        s = jnp.where(hit, NEG, s)
        vals.append(m)
    count = jnp.sum((rank < k).astype(F32), axis=0, keepdims=True)
    return vals, rank, count


def _route_body(hn_ref, wq_ref, k1_ref, k2_ref, e1w_ref, cnt_ref, e2_ref, rk2_ref, qh_ref):
    tm = hn_ref.shape[0]
    qh_ref[...] = _dot_nt(wq_ref[...], hn_ref[...])
    pairs = _pair_candidates()
    n_pad = (-len(pairs)) % 8

    def route_head(h, exact):
        base = pl.multiple_of(h * (2 * HALF_KEY), 2 * HALF_KEY)
        q1h, q1l = _split_bf16(qh_ref[pl.ds(base, HALF_KEY), :])
        q2h, q2l = _split_bf16(qh_ref[pl.ds(base + HALF_KEY, HALF_KEY), :])
        k1h, k1l = _split_bf16(k1_ref[h])
        k2h, k2l = _split_bf16(k2_ref[h])
        dot = functools.partial(jnp.dot, preferred_element_type=F32)
        s1 = dot(k1h, q1h) + (dot(k1h, q1l) + dot(k1l, q1h))
        s2 = dot(k2h, q2h) + (dot(k2h, q2l) + dot(k2l, q2h))
        v1, rk1, n1 = _top_rounds(s1, PEER_TOPK, exact)
        v2, rk2, n2 = _top_rounds(s2, PEER_TOPK, exact)
        cand = jnp.concatenate([v1[r1] + v2[r2] for r1, r2 in pairs]
                               + [jnp.full((n_pad, tm), NEG, F32)], axis=0)
        _, rkc, nc = _top_rounds(cand, PEER_TOPK, exact)
        picked = (rkc < PEER_TOPK).astype(F32)
        ez = jnp.exp(cand - cand[0:1, :]) * picked
        z = jnp.sum(ez, axis=0, keepdims=True)
        cnt = jnp.zeros((N_KEYS, tm), F32)
        for r1 in range(PEER_TOPK):
            rows = [i for i, (a, _) in enumerate(pairs) if a == r1]
            c_r1 = jnp.sum(picked[rows[0]:rows[-1] + 1, :], axis=0, keepdims=True)
            cnt = jnp.where(rk1 == r1, c_r1, cnt)
        e1w_ref[h] = jnp.exp(s1 - v1[0]) / z
        cnt_ref[h] = cnt
        e2_ref[h] = jnp.exp(s2 - v2[0]).astype(BF16)
        rk2_ref[h] = rk2.astype(F32).astype(BF16)
        return jnp.max(jnp.abs(n1 - PEER_TOPK) + jnp.abs(n2 - PEER_TOPK) + jnp.abs(nc - PEER_TOPK))

    def head(h, carry):
        n_tied = route_head(h, exact=False)

        @pl.when(n_tied > 0.0)
        def _():
            route_head(h, exact=True)

        return carry

    lax.fori_loop(0, PEER_HEADS, head, 0)


def _route(hn, wq_t, keys1, keys2, tm):
    m, d = hn.shape
    nq = wq_t.shape[0]
    out = jax.ShapeDtypeStruct((PEER_HEADS, N_KEYS, m), F32)
    outb = jax.ShapeDtypeStruct((PEER_HEADS, N_KEYS, m), BF16)
    oblk = pl.BlockSpec((PEER_HEADS, N_KEYS, tm), lambda i: (0, 0, i))
    kblk = pl.BlockSpec((PEER_HEADS, N_KEYS, HALF_KEY), lambda i: (0, 0, 0))
    return pl.pallas_call(
        _route_body,
        grid=(m // tm,),
        in_specs=[pl.BlockSpec((tm, d), lambda i: (i, 0)), pl.BlockSpec((nq, d), lambda i: (0, 0)),
                  kblk, kblk],
        out_specs=(oblk, oblk, oblk, oblk),
        out_shape=(out, out, outb, outb),
        scratch_shapes=[pltpu.VMEM((nq, tm), F32)],
        compiler_params=_cparams(("arbitrary",)),
        name="peer_route",
    )(hn, wq_t, keys1, keys2)


def _experts_body(hn_ref, u_ref, vt_ref, e1w_ref, cnt_ref, e2_ref, rk2_ref, h_ref, y_ref, acc_ref, *, n_a):
    e = pl.program_id(1)

    @pl.when(e == 0)
    def _():
        acc_ref[...] = jnp.zeros_like(acc_ref)

    tm = hn_ref.shape[0]
    zero = jnp.zeros((), BF16)
    hn = hn_ref[...]
    acc = acc_ref[...]
    for k0 in range(0, n_a, EXPERT_CHUNK_KEYS):
        rows = slice(k0 * N_KEYS, (k0 + EXPERT_CHUNK_KEYS) * N_KEYS)
        pre = _dot_nt(u_ref[rows, :], hn)
        act = (pre * 0.5 * (1.0 + lax.erf(pre * (2.0 ** -0.5)))).astype(BF16)
        parts = []
        for al in range(EXPERT_CHUNK_KEYS):
            a = e * n_a + k0 + al
            c16 = [jnp.broadcast_to(cnt_ref[h, pl.ds(a, 1), :], (BF16_ROWS, tm)).astype(BF16)
                   for h in range(PEER_HEADS)]
            w16 = [jnp.broadcast_to(e1w_ref[h, pl.ds(a, 1), :], (BF16_ROWS, tm)).astype(BF16)
                   for h in range(PEER_HEADS)]
            for r in range(0, N_KEYS, BF16_ROWS):
                g = None
                for h in range(PEER_HEADS):
                    t = jnp.where(rk2_ref[h, r:r + BF16_ROWS, :] < c16[h],
                                  e2_ref[h, r:r + BF16_ROWS, :], zero) * w16[h]
                    g = t if g is None else g + t
                parts.append(g * act[al * N_KEYS + r:al * N_KEYS + r + BF16_ROWS])
        p = jnp.concatenate(parts, axis=0)
        acc = acc + jnp.dot(vt_ref[:, rows], p, preferred_element_type=F32)
    acc_ref[...] = acc

    @pl.when(e == pl.num_programs(1) - 1)
    def _():
        y_ref[...] = h_ref[...] + acc_ref[...].T


def _experts(hn, u_b, vt_b, e1w, cnt, e2, rk2, h, tm, te):
    m, d = hn.shape
    ne = u_b.shape[0]
    rblk = pl.BlockSpec((PEER_HEADS, N_KEYS, tm), lambda i, e: (0, 0, i))
    return pl.pallas_call(
        functools.partial(_experts_body, n_a=te // N_KEYS),
        grid=(m // tm, ne // te),
        in_specs=[pl.BlockSpec((tm, d), lambda i, e: (i, 0)),
                  pl.BlockSpec((te, d), lambda i, e: (e, 0)),
                  pl.BlockSpec((d, te), lambda i, e: (0, e)),
                  rblk, rblk, rblk, rblk,
                  pl.BlockSpec((tm, d), lambda i, e: (i, 0))],
        out_specs=pl.BlockSpec((tm, d), lambda i, e: (i, 0)),
        out_shape=jax.ShapeDtypeStruct((m, d), F32),
        scratch_shapes=[pltpu.VMEM((d, tm), F32)],
        compiler_params=_cparams(("arbitrary", "arbitrary")),
        name="peer_experts",
    )(hn, u_b, vt_b, e1w, cnt, e2, rk2, h)


PAGES_PER_BLOCK = MOBA_BLOCK // PAGE


def _page_mean_body(k_ref, o_ref):
    o_ref[...] = jnp.sum(k_ref[...], axis=1) * (1.0 / PAGE)


def _page_means(kpool3, group):
    n_phys, page, w = kpool3.shape
    return pl.pallas_call(
        _page_mean_body,
        grid=(n_phys // group,),
        in_specs=[pl.BlockSpec((group, page, w), lambda i: (i, 0, 0))],
        out_specs=pl.BlockSpec((group, w), lambda i: (i, 0)),
        out_shape=jax.ShapeDtypeStruct((n_phys, w), F32),
        compiler_params=_cparams(("arbitrary",)),
        name="page_means",
    )(kpool3)


def _sample_select_body(pt_ref, q_ref, pm_ref, o_ref, km_ref, *, nb_past):
    b = pl.program_id(0)
    dt = q_ref.shape[1]

    def fill(j, carry):
        p0 = pt_ref[b, PAGES_PER_BLOCK * j]
        acc = pm_ref[pl.ds(p0, 1), :]
        for r in range(1, PAGES_PER_BLOCK):
            acc = acc + pm_ref[pl.ds(pt_ref[b, PAGES_PER_BLOCK * j + r], 1), :]
        km_ref[pl.ds(j, 1), :] = acc * (1.0 / PAGES_PER_BLOCK)
        return carry

    lax.fori_loop(0, nb_past, fill, 0)
    q = q_ref[0]
    lane_q = lax.broadcasted_iota(jnp.int32, q.shape, 1)
    qstack = jnp.concatenate([jnp.where(lane_q // HEAD_DIM == h, q, 0.0) for h in range(HEADS)], axis=0)
    gate = _dot_nt3(qstack, km_ref[...])
    lane = lax.broadcasted_iota(jnp.int32, gate.shape, 1)
    out_lane = lax.broadcasted_iota(jnp.int32, (gate.shape[0], LANES), 1)
    out = jnp.zeros((gate.shape[0], LANES), jnp.int32)
    for r in range(MOBA_TOPK):
        m = jnp.max(gate, axis=-1, keepdims=True)
        first = jnp.min(jnp.where(gate == m, lane, nb_past), axis=-1, keepdims=True)
        out = jnp.where(out_lane == r, first, out)
        gate = jnp.where(lane == first, NEG, gate)
    o_ref[0] = out


def _sample_select(q3, page_mean, page_table, nb_past):
    db, dt, w = q3.shape
    n_phys = page_mean.shape[0]
    rows = HEADS * dt
    return pl.pallas_call(
        functools.partial(_sample_select_body, nb_past=nb_past),
        grid_spec=pltpu.PrefetchScalarGridSpec(
            num_scalar_prefetch=1,
            grid=(db,),
            in_specs=[pl.BlockSpec((1, dt, w), lambda b, pt: (b, 0, 0)),
                      pl.BlockSpec((n_phys, w), lambda b, pt: (0, 0))],
            out_specs=pl.BlockSpec((1, rows, LANES), lambda b, pt: (b, 0, 0)),
            scratch_shapes=[pltpu.VMEM((nb_past, w), F32)]),
        out_shape=jax.ShapeDtypeStruct((db, rows, LANES), jnp.int32),
        compiler_params=_cparams(("arbitrary",)),
        name="sample_select",
    )(page_table, q3, page_mean)


def _sample_attend_body(ph_ref, q_ref, kn_ref, vn_ref, *refs, n_pick):
    dt = q_ref.shape[1]
    k_pages = refs[:dt * n_pick]
    v_pages = refs[dt * n_pick:2 * dt * n_pick]
    o_ref = refs[2 * dt * n_pick]
    half = pl.program_id(1) % (LANES // HEAD_DIM)
    lane = lax.broadcasted_iota(jnp.int32, (dt, LANES), 1)
    in_head = (lane // HEAD_DIM) == half
    qm = jnp.where(in_head, q_ref[0] * (HEAD_DIM ** -0.5), 0.0).astype(BF16)
    s_new = _dot_nt(qm, kn_ref[0].astype(BF16))
    tri = lax.broadcasted_iota(jnp.int32, (dt, dt), 1) <= lax.broadcasted_iota(jnp.int32, (dt, dt), 0)
    s_new = jnp.where(tri, s_new, NEG)
    rows = []
    for t in range(dt):
        s_pg = [_dot_nt(qm, k_pages[t * n_pick + n][0].astype(BF16))[t:t + 1] for n in range(n_pick)]
        s_loc = s_new[t:t + 1]
        m = jnp.max(s_loc, axis=-1, keepdims=True)
        for s in s_pg:
            m = jnp.maximum(m, jnp.max(s, axis=-1, keepdims=True))
        p_loc = jnp.exp(s_loc - m)
        den = jnp.sum(p_loc, axis=-1, keepdims=True)
        num = jnp.dot(p_loc.astype(BF16), vn_ref[0].astype(BF16), preferred_element_type=F32)
        for n, s in enumerate(s_pg):
            p = jnp.exp(s - m)
            den = den + jnp.sum(p, axis=-1, keepdims=True)
            num = num + jnp.dot(p.astype(BF16), v_pages[t * n_pick + n][0].astype(BF16),
                                preferred_element_type=F32)
        rows.append(num / den)
    new = jnp.concatenate(rows, axis=0)

    @pl.when(half == 0)
    def _():
        o_ref[0] = new

    @pl.when(half != 0)
    def _():
        o_ref[0] = jnp.where(in_head, new, o_ref[0])


def _sample_attend(q3, kn3, vn3, kpool3, vpool3, phys):
    db, dt, w = q3.shape
    n_pick = phys.shape[1] // (dt * HEADS)
    hps = LANES // HEAD_DIM
    new_blk = pl.BlockSpec((1, dt, LANES), lambda b, h, ph: (b, 0, h // hps))

    def page_spec(t, n):
        return pl.BlockSpec((1, PAGE, LANES),
                            lambda b, h, ph: (ph[b, (t * HEADS + h) * n_pick + n], 0, h // hps))

    pages = [page_spec(t, n) for t in range(dt) for n in range(n_pick)]
    return pl.pallas_call(
        functools.partial(_sample_attend_body, n_pick=n_pick),
        grid_spec=pltpu.PrefetchScalarGridSpec(
            num_scalar_prefetch=1,
            grid=(db, HEADS),
            in_specs=[new_blk, new_blk, new_blk] + pages + pages,
            out_specs=new_blk),
        out_shape=jax.ShapeDtypeStruct((db, dt, w), F32),
        compiler_params=_cparams(("arbitrary", "arbitrary")),
        name="sample_attend",
    )(phys, q3, kn3, vn3, *([kpool3] * len(pages)), *([vpool3] * len(pages)))


def _moba_sample(q3, kn3, vn3, k_pool, v_pool, page_table):
    db, dt, w = q3.shape
    n_phys = k_pool.shape[0]
    nb_past = (page_table.shape[1] * PAGE) // MOBA_BLOCK
    assert nb_past >= MOBA_TOPK
    kpool3 = k_pool.reshape(n_phys, PAGE, w)
    vpool3 = v_pool.reshape(n_phys, PAGE, w)
    group = max(g for g in range(1, 33) if n_phys % g == 0 and (g % 8 == 0 or g == n_phys))
    sel = _sample_select(q3, _page_means(kpool3, group), page_table, nb_past)
    sel = sel[:, :, :MOBA_TOPK].reshape(db, HEADS, dt, MOBA_TOPK).transpose(0, 2, 1, 3)
    logical = sel[..., None] * PAGES_PER_BLOCK + jnp.arange(PAGES_PER_BLOCK, dtype=jnp.int32)
    phys = jnp.take_along_axis(page_table, logical.reshape(db, -1), axis=1)
    return _sample_attend(q3, kn3, vn3, kpool3, vpool3, phys)


def _mixer_tail(x2d, attn2d, conv2d, w_out_b, g2, wq_t, keys1, keys2, u_b, vt_b, tm, tr, tx, te):
    h, hn = _back(x2d, attn2d, conv2d, w_out_b, g2, tm)
    routing = _route(hn, wq_t, keys1, keys2, tr)
    return _experts(hn, u_b, vt_b, *routing, h, tx, te)


TILE_M = 512
TILE_ROUTE = 256
TILE_X = 512
TILE_E = 1024
CONV_CHUNK = 64


def kernel(x_prompt, x_sample, cache_k, cache_v, state_conv, page_table, norm1_g, w_in, q_norm_g, k_norm_g, conv_w, conv_b, conv_norm_g, w_out, norm2_g, peer_w_query, peer_keys1, peer_keys2, peer_u, peer_v):
    depth = w_in.shape[0]
    bsz, seq, d = x_prompt.shape
    db, dt, _ = x_sample.shape
    assert seq % TILE_M == 0 and TILE_M % MOBA_BLOCK == 0 and d - ATTN_W == ATTN_W
    assert (page_table.shape[1] * PAGE) % MOBA_BLOCK == 0, "partially filled own block is not supported"
    bd = _head_blockdiag()
    xp, xs = x_prompt, x_sample
    rows = {n: [] for n in ("kp", "vp", "cp", "ks", "vs", "cs")}
    for l in range(depth):
        w_in_b = w_in[l].astype(BF16)
        w_out_b = w_out[l].astype(BF16)
        wq_t = peer_w_query[l].T.astype(BF16)
        u_b = peer_u[l].astype(BF16)
        vt_b = peer_v[l].T.astype(BF16)
        g1 = norm1_g[l][None]
        g2 = norm2_g[l][None]
        qg = q_norm_g[l].reshape(1, ATTN_W)
        kg = k_norm_g[l].reshape(1, ATTN_W)
        cb = conv_b[l][None]
        cg = conv_norm_g[l][None]
        tail = functools.partial(_mixer_tail, w_out_b=w_out_b, g2=g2, wq_t=wq_t, keys1=peer_keys1[l],
                                 keys2=peer_keys2[l], u_b=u_b, vt_b=vt_b, te=TILE_E)

        x2d = xp.reshape(bsz * seq, d)
        q, k, v, u, kb, vb, km = _front(x2d, g1, w_in_b, qg, kg, bd, TILE_M, TILE_M // MOBA_BLOCK)
        r3 = lambda t: t.reshape(bsz, seq, ATTN_W)
        attn = _moba_prompt(r3(q), r3(kb), r3(vb), km.reshape(bsz, seq // MOBA_BLOCK, ATTN_W))
        conv = _conv(r3(u), jnp.zeros((bsz, HIST, ATTN_W), F32), conv_w[l], cb, cg, TILE_M, CONV_CHUNK)
        y = tail(x2d, attn.reshape(-1, ATTN_W), conv.reshape(-1, ATTN_W),
                 tm=TILE_M, tr=TILE_ROUTE, tx=TILE_X)
        xp = y.reshape(bsz, seq, d)
        rows["kp"].append(k.reshape(bsz, seq, HEADS, HEAD_DIM))
        rows["vp"].append(v.reshape(bsz, seq, HEADS, HEAD_DIM))
        rows["cp"].append(r3(u)[:, -(CONV_K - 1):])

        n_s = db * dt
        x2d = xs.reshape(n_s, d)
        q, k, v, u, _, _, _ = _front(x2d, g1, w_in_b, qg, kg, bd, n_s, 1)
        h4 = lambda t: t.reshape(db, dt, HEADS, HEAD_DIM)
        s3 = lambda t: t.reshape(db, dt, ATTN_W)
        attn = _moba_sample(s3(q), s3(k), s3(v), cache_k[l], cache_v[l], page_table)
        u3 = s3(u)
        hist = jnp.concatenate([jnp.zeros((db, HIST - (CONV_K - 1), ATTN_W), F32), state_conv[l]], axis=1)
        conv = _conv(u3, hist, conv_w[l], cb, cg, dt, dt)
        y = tail(x2d, attn.reshape(n_s, ATTN_W), conv.reshape(n_s, ATTN_W), tm=n_s, tr=n_s, tx=n_s)
        xs = y.reshape(db, dt, d)
        rows["ks"].append(h4(k))
        rows["vs"].append(h4(v))
        rows["cs"].append(jnp.concatenate([state_conv[l], u3], axis=1)[:, -(CONV_K - 1):])
    return (xp, xs, jnp.stack(rows["kp"]), jnp.stack(rows["vp"]), jnp.stack(rows["cp"]),
            jnp.stack(rows["ks"]), jnp.stack(rows["vs"]), jnp.stack(rows["cs"]))
```

```python
import functools

import jax
import jax.numpy as jnp
from jax import lax
from jax.experimental import pallas as pl
from jax.experimental.pallas import tpu as pltpu

F32 = jnp.float32
BF16 = jnp.bfloat16

HEADS = 8
HEAD_DIM = 64
ATTN_W = HEADS * HEAD_DIM
CONV_K = 31
MOBA_BLOCK = 256
MOBA_TOPK = 3
PAGE = 128
PEER_HEADS = 8
N_KEYS = 128
PEER_TOPK = 16
HALF_KEY = 128
EPS = 1e-6
NEG = float("-inf")

LANES = 128
BF16_ROWS = 16
ROUTE_UNROLL = 2
VMEM_LIMIT = 56 * 1024 * 1024


def _cparams(sem):
    return pltpu.CompilerParams(dimension_semantics=sem, vmem_limit_bytes=VMEM_LIMIT)


def _split_bf16(x):
    hi = x.astype(BF16)
    lo = (x - hi.astype(F32)).astype(BF16)
    return hi, lo


def _dot_nt(a, b):
    return lax.dot_general(a, b, (((1,), (1,)), ((), ())), preferred_element_type=F32)


def _dot_nt3(a, b):
    ah, al = _split_bf16(a)
    bh, bl = _split_bf16(b)
    return _dot_nt(ah, bh) + (_dot_nt(ah, bl) + _dot_nt(al, bh))


def _head_blockdiag():
    r = lax.broadcasted_iota(jnp.int32, (ATTN_W, ATTN_W), 0) // HEAD_DIM
    c = lax.broadcasted_iota(jnp.int32, (ATTN_W, ATTN_W), 1) // HEAD_DIM
    return jnp.where(r == c, 1.0 / HEAD_DIM, 0.0).astype(BF16)


def _front_body(x_ref, g1_ref, w_ref, qg_ref, kg_ref, bd_ref,
                q_ref, k_ref, v_ref, u_ref, kb_ref, vb_ref, km_ref, *, n_sub):
    x = x_ref[...]
    ms = jnp.mean(x * x, axis=-1, keepdims=True)
    xn = (x * lax.rsqrt(ms + EPS)) * g1_ref[...]
    proj = jnp.dot(xn.astype(BF16), w_ref[...], preferred_element_type=F32)
    bd = bd_ref[...]

    def head_norm(t, g):
        hi, lo = _split_bf16(t * t)
        msq = jnp.dot(hi, bd, preferred_element_type=F32) + jnp.dot(lo, bd, preferred_element_type=F32)
        return (t * lax.rsqrt(msq + EPS)) * g

    q_ref[...] = head_norm(proj[:, :ATTN_W], qg_ref[...])
    k = head_norm(proj[:, ATTN_W:2 * ATTN_W], kg_ref[...])
    v = proj[:, 2 * ATTN_W:3 * ATTN_W]
    k_ref[...] = k
    v_ref[...] = v
    kb_ref[...] = k.astype(BF16)
    vb_ref[...] = v.astype(BF16)
    a = proj[:, 3 * ATTN_W:3 * ATTN_W + ATTN_W]
    gt = proj[:, 4 * ATTN_W:]
    u_ref[...] = a * jax.nn.sigmoid(gt)
    rows = k.shape[0] // n_sub
    for s in range(n_sub):
        km_ref[0, s:s + 1, :] = jnp.sum(k[s * rows:(s + 1) * rows], axis=0, keepdims=True) * (1.0 / rows)


def _front(x2d, g1, w_in_b, qg, kg, bd, tm, n_sub):
    m, d = x2d.shape
    nw = w_in_b.shape[1]
    row = lambda i: (i, 0)
    fixed = lambda i: (0, 0)
    outs = (
        jax.ShapeDtypeStruct((m, ATTN_W), F32),
        jax.ShapeDtypeStruct((m, ATTN_W), F32),
        jax.ShapeDtypeStruct((m, ATTN_W), F32),
        jax.ShapeDtypeStruct((m, ATTN_W), F32),
        jax.ShapeDtypeStruct((m, ATTN_W), BF16),
        jax.ShapeDtypeStruct((m, ATTN_W), BF16),
        jax.ShapeDtypeStruct((m // tm, n_sub, ATTN_W), F32),
    )
    blk = pl.BlockSpec((tm, ATTN_W), row)
    return pl.pallas_call(
        functools.partial(_front_body, n_sub=n_sub),
        grid=(m // tm,),
        in_specs=[pl.BlockSpec((tm, d), row), pl.BlockSpec((1, d), fixed),
                  pl.BlockSpec((d, nw), fixed), pl.BlockSpec((1, ATTN_W), fixed),
                  pl.BlockSpec((1, ATTN_W), fixed), pl.BlockSpec((ATTN_W, ATTN_W), fixed)],
        out_specs=(blk, blk, blk, blk, blk, blk,
                   pl.BlockSpec((1, n_sub, ATTN_W), lambda i: (i, 0, 0))),
        out_shape=outs,
        compiler_params=_cparams(("arbitrary",)),
        name="front",
    )(x2d, g1, w_in_b, qg, kg, bd)


HIST = 32


def _conv_body(u_ref, h0_ref, w_ref, b_ref, g_ref, o_ref, buf_ref, *, ts, chunk):
    t = pl.program_id(1)

    @pl.when(t == 0)
    def _():
        buf_ref[0:HIST, :] = h0_ref[0]

    @pl.when(t != 0)
    def _():
        buf_ref[0:HIST, :] = buf_ref[ts:ts + HIST, :]

    buf_ref[HIST:HIST + ts, :] = u_ref[0]
    w = w_ref[...]
    off = HIST - (CONV_K - 1)
    for c in range(ts // chunk):
        acc = jnp.zeros((chunk, u_ref.shape[2]), F32)
        for j in range(CONV_K):
            acc = acc + buf_ref[c * chunk + off + j:c * chunk + off + j + chunk, :] * w[j:j + 1, :]
        y = acc + b_ref[...]
        ms = jnp.mean(y * y, axis=-1, keepdims=True)
        y = (y * lax.rsqrt(ms + EPS)) * g_ref[...]
        o_ref[0, c * chunk:(c + 1) * chunk, :] = y * jax.nn.sigmoid(y)


def _conv(u3, hist0, conv_w, conv_b, conv_g, ts, chunk):
    b, l, c = u3.shape
    fixed = lambda i, t: (0, 0)
    return pl.pallas_call(
        functools.partial(_conv_body, ts=ts, chunk=chunk),
        grid=(b, l // ts),
        in_specs=[pl.BlockSpec((1, ts, c), lambda i, t: (i, t, 0)),
                  pl.BlockSpec((1, HIST, c), lambda i, t: (i, 0, 0)),
                  pl.BlockSpec((CONV_K, c), fixed), pl.BlockSpec((1, c), fixed),
                  pl.BlockSpec((1, c), fixed)],
        out_specs=pl.BlockSpec((1, ts, c), lambda i, t: (i, t, 0)),
        out_shape=jax.ShapeDtypeStruct((b, l, c), F32),
        scratch_shapes=[pltpu.VMEM((ts + HIST, c), F32)],
        compiler_params=_cparams(("arbitrary", "arbitrary")),
        name="conv_branch",
    )(u3, hist0, conv_w, conv_b, conv_g)


def _block_select(gate, blk_idx, n_valid, nb):
    height = gate.shape[0]
    valid = blk_idx < n_valid
    g = jnp.where(valid, gate, NEG)
    rank = jnp.zeros(gate.shape, jnp.int32)
    for s in range(1, nb):
        fwd = pltpu.roll(g, height - s, 0)
        bwd = pltpu.roll(g, nb - s, 0)
        wraps = blk_idx + s >= nb
        other = jnp.where(wraps, bwd, fwd)
        beats = (other > g) | ((other == g) & wraps)
        rank = rank + beats.astype(jnp.int32)
    return valid & (rank < MOBA_TOPK)


MASK_BIAS = -1e30
ATTN_GROUP = 4


def _attn_body(q_ref, kb_ref, vb_ref, km_ref, ind_ref, o_ref, s_ref, mx_ref, l_ref, acc_ref, *, nb):
    i = pl.program_id(1)
    blk = MOBA_BLOCK
    hps = LANES // HEAD_DIM
    q = q_ref[0]
    km = km_ref[0]
    pad_rows = LANES - HEADS * nb
    kmt = jnp.concatenate([km] * HEADS + ([jnp.zeros((pad_rows, ATTN_W), F32)] if pad_rows else []), axis=0)
    r_head = lax.broadcasted_iota(jnp.int32, kmt.shape, 0) // nb
    c_head = lax.broadcasted_iota(jnp.int32, kmt.shape, 1) // HEAD_DIM
    kmt = jnp.where(r_head == c_head, kmt, 0.0)
    gate_t = _dot_nt3(kmt, q)
    rix = lax.broadcasted_iota(jnp.int32, (LANES, blk), 0)
    n_valid = jnp.where(rix < HEADS * nb, i, 0)
    sel_t = _block_select(gate_t, rix % nb, n_valid, nb)
    bias = jnp.where(sel_t, 0.0, MASK_BIAS).T
    lane = lax.broadcasted_iota(jnp.int32, (blk, LANES), 1)

    row = lax.broadcasted_iota(jnp.int32, (hps * blk, blk), 0) % blk
    col = lax.broadcasted_iota(jnp.int32, (hps * blk, blk), 1)
    causal = col <= row
    scale = HEAD_DIM ** -0.5
    own0 = pl.multiple_of(i * blk, blk)
    grp = ATTN_GROUP * blk
    n_grp = (i + ATTN_GROUP) // ATTN_GROUP

    def tilemax(s):
        out = s[:, :LANES]
        for t in range(1, s.shape[1] // LANES):
            out = jnp.maximum(out, s[:, t * LANES:(t + 1) * LANES])
        return out

    def tilesum(s):
        out = s[:, :LANES]
        for t in range(1, s.shape[1] // LANES):
            out = out + s[:, t * LANES:(t + 1) * LANES]
        return out

    for c in range(ATTN_W // LANES):
        slab = slice(c * LANES, (c + 1) * LANES)
        qs = q[:, slab] * scale
        stacked = []
        for half in range(hps):
            h = c * hps + half
            qm = jnp.where((lane // HEAD_DIM) == half, qs, 0.0).astype(BF16)
            bh = jnp.where((lane // nb) == h, bias, 0.0).astype(BF16)
            stacked.append(jnp.concatenate([qm, bh], axis=1))
        q2 = jnp.concatenate(stacked, axis=0)

        mx_ref[...] = jnp.full(mx_ref.shape, NEG, F32)

        def scores(g, carry, q2=q2, slab=slab):
            g0 = pl.multiple_of(g * grp, grp)
            kj = jnp.concatenate([kb_ref[0, pl.ds(g0, grp), slab], ind_ref[pl.ds(g0, grp), :]], axis=1)
            s = _dot_nt(q2, kj)
            s_ref[:, pl.ds(g0, grp)] = s
            mx_ref[...] = jnp.maximum(mx_ref[...], tilemax(s))
            return carry

        lax.fori_loop(0, n_grp, scores, 0)
        s = jnp.where(causal, _dot_nt(q2[:, :LANES], kb_ref[0, pl.ds(own0, blk), slab]), NEG)
        s_ref[:, pl.ds(own0, blk)] = s
        m = jnp.max(jnp.maximum(mx_ref[...], tilemax(s)), axis=-1, keepdims=True)
        l_ref[...] = jnp.zeros_like(l_ref)
        acc_ref[...] = jnp.zeros_like(acc_ref)

        def weigh(g, carry, m=m, slab=slab):
            g0 = pl.multiple_of(g * grp, grp)
            p = jnp.exp(s_ref[:, pl.ds(g0, grp)] - m)
            l_ref[...] += tilesum(p)
            acc_ref[...] += jnp.dot(p.astype(BF16), vb_ref[0, pl.ds(g0, grp), slab],
                                    preferred_element_type=F32)
            return carry

        lax.fori_loop(0, n_grp, weigh, 0)
        o = acc_ref[...] / jnp.sum(l_ref[...], axis=-1, keepdims=True)
        out = o[0:blk]
        for half in range(1, hps):
            out = jnp.where((lane // HEAD_DIM) == half, o[half * blk:(half + 1) * blk], out)
        o_ref[0, :, slab] = out


def _moba_prompt(q3, kb3, vb3, km3):
    b, s, w = q3.shape
    nb = s // MOBA_BLOCK
    assert HEADS * nb <= LANES and nb % ATTN_GROUP == 0
    rows = (LANES // HEAD_DIM) * MOBA_BLOCK
    key_blk = lax.broadcasted_iota(jnp.int32, (s, LANES), 0) // MOBA_BLOCK
    lane = lax.broadcasted_iota(jnp.int32, (s, LANES), 1)
    ind = ((key_blk == lane % nb) & (lane < HEADS * nb)).astype(BF16)
    return pl.pallas_call(
        functools.partial(_attn_body, nb=nb),
        grid=(b, nb),
        in_specs=[pl.BlockSpec((1, MOBA_BLOCK, w), lambda bi, i: (bi, i, 0)),
                  pl.BlockSpec((1, s, w), lambda bi, i: (bi, 0, 0)),
                  pl.BlockSpec((1, s, w), lambda bi, i: (bi, 0, 0)),
                  pl.BlockSpec((1, nb, w), lambda bi, i: (bi, 0, 0)),
                  pl.BlockSpec((s, LANES), lambda bi, i: (0, 0))],
        out_specs=pl.BlockSpec((1, MOBA_BLOCK, w), lambda bi, i: (bi, i, 0)),
        out_shape=jax.ShapeDtypeStruct((b, s, w), F32),
        scratch_shapes=[pltpu.VMEM((rows, s), F32), pltpu.VMEM((rows, LANES), F32),
                        pltpu.VMEM((rows, LANES), F32), pltpu.VMEM((rows, LANES), F32)],
        compiler_params=_cparams(("arbitrary", "arbitrary")),
        name="moba_prompt",
    )(q3, kb3, vb3, km3, ind)


def _back_body(x_ref, a_ref, c_ref, w_ref, g2_ref, h_ref, hn_ref):
    mix = jnp.concatenate([a_ref[...], c_ref[...]], axis=-1).astype(BF16)
    h = x_ref[...] + jnp.dot(mix, w_ref[...], preferred_element_type=F32)
    h_ref[...] = h
    ms = jnp.mean(h * h, axis=-1, keepdims=True)
    hn_ref[...] = ((h * lax.rsqrt(ms + EPS)) * g2_ref[...]).astype(BF16)


def _back(x2d, attn2d, conv2d, w_out_b, g2, tm):
    m, d = x2d.shape
    row = lambda i: (i, 0)
    fixed = lambda i: (0, 0)
    return pl.pallas_call(
        _back_body,
        grid=(m // tm,),
        in_specs=[pl.BlockSpec((tm, d), row), pl.BlockSpec((tm, ATTN_W), row),
                  pl.BlockSpec((tm, d - ATTN_W), row), pl.BlockSpec((d, d), fixed),
                  pl.BlockSpec((1, d), fixed)],
        out_specs=(pl.BlockSpec((tm, d), row), pl.BlockSpec((tm, d), row)),
        out_shape=(jax.ShapeDtypeStruct((m, d), F32), jax.ShapeDtypeStruct((m, d), BF16)),
        compiler_params=_cparams(("arbitrary",)),
        name="back",
    )(x2d, attn2d, conv2d, w_out_b, g2)


def _pair_candidates():
    return [(r1, r2) for r1 in range(PEER_TOPK) for r2 in range(PEER_TOPK)
            if (r1 + 1) * (r2 + 1) <= PEER_TOPK]


def _top_rounds(s, k, exact):
    n = s.shape[0]
    idx = lax.broadcasted_iota(jnp.int32, s.shape, 0)
    rank = jnp.full(s.shape, k, jnp.int32)
    vals = []
    for r in range(k):
        m = jnp.max(s, axis=0, keepdims=True)
        hit = s == m
        if exact:
            first = jnp.min(jnp.where(hit, idx, n), axis=0, keepdims=True)
            hit = idx == first
        rank = jnp.where(hit, r, rank)
        s = jnp.where(hit, NEG, s)
        vals.append(m)
    count = jnp.sum((rank < k).astype(F32), axis=0, keepdims=True)
    return vals, rank, count


def _route_body(hn_ref, wq_ref, k1_ref, k2_ref, e1w_ref, cnt_ref, e2_ref, rk2_ref, qh_ref):
    tm = hn_ref.shape[0]
    qh_ref[...] = _dot_nt(wq_ref[...], hn_ref[...])
    pairs = _pair_candidates()
    n_pad = (-len(pairs)) % 8

    def route_head(h, exact):
        base = pl.multiple_of(h * (2 * HALF_KEY), 2 * HALF_KEY)
        q1h, q1l = _split_bf16(qh_ref[pl.ds(base, HALF_KEY), :])
        q2h, q2l = _split_bf16(qh_ref[pl.ds(base + HALF_KEY, HALF_KEY), :])
        k1h, k1l = _split_bf16(k1_ref[h])
        k2h, k2l = _split_bf16(k2_ref[h])
        dot = functools.partial(jnp.dot, preferred_element_type=F32)
        s1 = dot(k1h, q1h) + (dot(k1h, q1l) + dot(k1l, q1h))
        s2 = dot(k2h, q2h) + (dot(k2h, q2l) + dot(k2l, q2h))
        v1, rk1, n1 = _top_rounds(s1, PEER_TOPK, exact)
        v2, rk2, n2 = _top_rounds(s2, PEER_TOPK, exact)
        cand = jnp.concatenate([v1[r1] + v2[r2] for r1, r2 in pairs]
                               + [jnp.full((n_pad, tm), NEG, F32)], axis=0)
        _, rkc, nc = _top_rounds(cand, PEER_TOPK, exact)
        picked = (rkc < PEER_TOPK).astype(F32)
        ez = jnp.exp(cand - cand[0:1, :]) * picked
        z = jnp.sum(ez, axis=0, keepdims=True)
        cnt = jnp.zeros((N_KEYS, tm), F32)
        for r1 in range(PEER_TOPK):
            rows = [i for i, (a, _) in enumerate(pairs) if a == r1]
            c_r1 = jnp.sum(picked[rows[0]:rows[-1] + 1, :], axis=0, keepdims=True)
            cnt = jnp.where(rk1 == r1, c_r1, cnt)
        e1w_ref[h] = jnp.exp(s1 - v1[0]) / z
        cnt_ref[h] = cnt
        e2_ref[h] = jnp.exp(s2 - v2[0]).astype(BF16)
        rk2_ref[h] = rk2.astype(F32).astype(BF16)
        return jnp.max(jnp.abs(n1 - PEER_TOPK) + jnp.abs(n2 - PEER_TOPK) + jnp.abs(nc - PEER_TOPK))

    def heads(i, carry):
        hs = [i * ROUTE_UNROLL + k for k in range(ROUTE_UNROLL)]
        n_tied = [route_head(h, exact=False) for h in hs]
        for h, n in zip(hs, n_tied):
            @pl.when(n > 0.0)
            def _(h=h):
                route_head(h, exact=True)

        return carry

    lax.fori_loop(0, PEER_HEADS // ROUTE_UNROLL, heads, 0)


def _route(hn, wq_t, keys1, keys2, tm):
    m, d = hn.shape
    nq = wq_t.shape[0]
    out = jax.ShapeDtypeStruct((PEER_HEADS, N_KEYS, m), F32)
    outb = jax.ShapeDtypeStruct((PEER_HEADS, N_KEYS, m), BF16)
    oblk = pl.BlockSpec((PEER_HEADS, N_KEYS, tm), lambda i: (0, 0, i))
    kblk = pl.BlockSpec((PEER_HEADS, N_KEYS, HALF_KEY), lambda i: (0, 0, 0))
    return pl.pallas_call(
        _route_body,
        grid=(m // tm,),
        in_specs=[pl.BlockSpec((tm, d), lambda i: (i, 0)), pl.BlockSpec((nq, d), lambda i: (0, 0)),
                  kblk, kblk],
        out_specs=(oblk, oblk, oblk, oblk),
        out_shape=(out, out, outb, outb),
        scratch_shapes=[pltpu.VMEM((nq, tm), F32)],
        compiler_params=_cparams(("arbitrary",)),
        name="peer_route",
    )(hn, wq_t, keys1, keys2)


def _experts_body(hn_ref, u_ref, vt_ref, e1w_ref, cnt_ref, e2_ref, rk2_ref, h_ref, y_ref, acc_ref, *, n_a):
    e = pl.program_id(1)

    @pl.when(e == 0)
    def _():
        acc_ref[...] = jnp.zeros_like(acc_ref)

    tm = hn_ref.shape[0]
    zero = jnp.zeros((), BF16)
    pre = _dot_nt(u_ref[...], hn_ref[...])
    act = (pre * 0.5 * (1.0 + lax.erf(pre * (2.0 ** -0.5)))).astype(BF16)
    parts = []
    for al in range(n_a):
        a = e * n_a + al
        c16 = [jnp.broadcast_to(cnt_ref[h, pl.ds(a, 1), :], (BF16_ROWS, tm)).astype(BF16)
               for h in range(PEER_HEADS)]
        w16 = [jnp.broadcast_to(e1w_ref[h, pl.ds(a, 1), :], (BF16_ROWS, tm)).astype(BF16)
               for h in range(PEER_HEADS)]
        for r in range(0, N_KEYS, BF16_ROWS):
            g = None
            for h in range(PEER_HEADS):
                t = jnp.where(rk2_ref[h, r:r + BF16_ROWS, :] < c16[h],
                              e2_ref[h, r:r + BF16_ROWS, :], zero) * w16[h]
                g = t if g is None else g + t
            parts.append(g * act[al * N_KEYS + r:al * N_KEYS + r + BF16_ROWS])
    p = jnp.concatenate(parts, axis=0)
    acc_ref[...] += jnp.dot(vt_ref[...], p, preferred_element_type=F32)

    @pl.when(e == pl.num_programs(1) - 1)
    def _():
        y_ref[...] = h_ref[...] + acc_ref[...].T


def _experts(hn, u_b, vt_b, e1w, cnt, e2, rk2, h, tm, te):
    m, d = hn.shape
    ne = u_b.shape[0]
    rblk = pl.BlockSpec((PEER_HEADS, N_KEYS, tm), lambda i, e: (0, 0, i))
    return pl.pallas_call(
        functools.partial(_experts_body, n_a=te // N_KEYS),
        grid=(m // tm, ne // te),
        in_specs=[pl.BlockSpec((tm, d), lambda i, e: (i, 0)),
                  pl.BlockSpec((te, d), lambda i, e: (e, 0)),
                  pl.BlockSpec((d, te), lambda i, e: (0, e)),
                  rblk, rblk, rblk, rblk,
                  pl.BlockSpec((tm, d), lambda i, e: (i, 0))],
        out_specs=pl.BlockSpec((tm, d), lambda i, e: (i, 0)),
        out_shape=jax.ShapeDtypeStruct((m, d), F32),
        scratch_shapes=[pltpu.VMEM((d, tm), F32)],
        compiler_params=_cparams(("arbitrary", "arbitrary")),
        name="peer_experts",
    )(hn, u_b, vt_b, e1w, cnt, e2, rk2, h)


PAGES_PER_BLOCK = MOBA_BLOCK // PAGE


def _same_head(n_rows, n_cols):
    r = lax.broadcasted_iota(jnp.int32, (n_rows, n_cols), 0) % HEADS
    c = lax.broadcasted_iota(jnp.int32, (n_rows, n_cols), 1) % HEADS
    return r == c


def _sample_body(pt_ref, q_ref, kn_ref, vn_ref, *refs, nbp):
    k_pages = refs[:PAGES_PER_BLOCK]
    v_pages = refs[PAGES_PER_BLOCK:2 * PAGES_PER_BLOCK]
    o_ref, g_s, m_s, l_s, acc_s = refs[2 * PAGES_PER_BLOCK:]
    j = pl.program_id(1)
    dt = q_ref.shape[1]
    rows = dt * HEADS
    q = q_ref[0].reshape(rows, HEAD_DIM)
    qs = (q * (HEAD_DIM ** -0.5)).astype(BF16)

    kblk = jnp.concatenate([r[...] for r in k_pages], axis=0)
    vblk = jnp.concatenate([r[...] for r in v_pages], axis=0)
    kmean = jnp.sum(kblk, axis=0) * (1.0 / MOBA_BLOCK)
    k2 = kblk.reshape(MOBA_BLOCK * HEADS, HEAD_DIM).astype(BF16)
    v2 = vblk.reshape(MOBA_BLOCK * HEADS, HEAD_DIM).astype(BF16)
    s = jnp.where(_same_head(rows, MOBA_BLOCK * HEADS), _dot_nt(qs, k2), NEG)
    m = jnp.max(s, axis=-1, keepdims=True)
    p = jnp.exp(s - m)
    gate_all = _dot_nt3(q, kmean)
    gate = jnp.sum(jnp.where(_same_head(rows, HEADS), gate_all, 0.0), axis=-1, keepdims=True)
    wide = (rows, LANES)
    g_s[j] = jnp.broadcast_to(gate, wide)
    m_s[j] = jnp.broadcast_to(m, wide)
    l_s[j] = jnp.broadcast_to(jnp.sum(p, axis=-1, keepdims=True), wide)
    acc_s[j] = jnp.dot(p.astype(BF16), v2, preferred_element_type=F32)

    @pl.when(j == nbp - 1)
    def _():
        gates = g_s[...]
        idx = lax.broadcasted_iota(jnp.int32, gates.shape, 0)
        picked = jnp.zeros(gates.shape, jnp.bool_)
        for _ in range(MOBA_TOPK):
            top = jnp.max(gates, axis=0, keepdims=True)
            first = jnp.min(jnp.where(gates == top, idx, nbp), axis=0, keepdims=True)
            hit = idx == first
            picked = picked | hit
            gates = jnp.where(hit, NEG, gates)
        kn2 = kn_ref[0].reshape(rows, HEAD_DIM).astype(BF16)
        vn2 = vn_ref[0].reshape(rows, HEAD_DIM).astype(BF16)
        r_t = lax.broadcasted_iota(jnp.int32, (rows, rows), 0) // HEADS
        c_t = lax.broadcasted_iota(jnp.int32, (rows, rows), 1) // HEADS
        sn = jnp.where(_same_head(rows, rows) & (c_t <= r_t), _dot_nt(qs, kn2), NEG)
        mn = jnp.max(sn, axis=-1, keepdims=True)
        pn = jnp.exp(sn - mn)
        ln = jnp.sum(pn, axis=-1, keepdims=True)
        an = jnp.dot(pn.astype(BF16), vn2, preferred_element_type=F32)
        m_all = m_s[...]
        m_tot = jnp.maximum(jnp.max(jnp.where(picked, m_all, NEG), axis=0), mn)
        w = jnp.where(picked, jnp.exp(m_all - m_tot), 0.0)
        wn = jnp.exp(mn - m_tot)
        den = jnp.sum(w * l_s[...], axis=0) + wn * ln
        num = jnp.sum(w[:, :, :HEAD_DIM] * acc_s[...], axis=0) + wn[:, :HEAD_DIM] * an
        o_ref[0] = (num / den[:, :HEAD_DIM]).reshape(dt, HEADS, HEAD_DIM)


def _moba_sample(q4, kn4, vn4, cache_k, cache_v, page_table, layer):
    db, dt = q4.shape[:2]
    nbp = (page_table.shape[1] * PAGE) // MOBA_BLOCK
    assert nbp >= MOBA_TOPK
    rows = dt * HEADS
    new_blk = pl.BlockSpec((1, dt, HEADS, HEAD_DIM), lambda b, j, pt: (b, 0, 0, 0))

    def page_spec(r):
        return pl.BlockSpec((None, None, PAGE, HEADS, HEAD_DIM),
                            lambda b, j, pt: (layer, pt[b, PAGES_PER_BLOCK * j + r], 0, 0, 0))

    pages = [page_spec(r) for r in range(PAGES_PER_BLOCK)]
    return pl.pallas_call(
        functools.partial(_sample_body, nbp=nbp),
        grid_spec=pltpu.PrefetchScalarGridSpec(
            num_scalar_prefetch=1,
            grid=(db, nbp),
            in_specs=[new_blk, new_blk, new_blk] + pages + pages,
            out_specs=new_blk,
            scratch_shapes=[pltpu.VMEM((nbp, rows, LANES), F32), pltpu.VMEM((nbp, rows, LANES), F32),
                            pltpu.VMEM((nbp, rows, LANES), F32), pltpu.VMEM((nbp, rows, HEAD_DIM), F32)]),
        out_shape=jax.ShapeDtypeStruct((db, dt, HEADS, HEAD_DIM), F32),
        compiler_params=_cparams(("arbitrary", "arbitrary")),
        name="moba_sample",
    )(page_table, q4, kn4, vn4, *([cache_k] * PAGES_PER_BLOCK), *([cache_v] * PAGES_PER_BLOCK))


def _mixer_tail(x2d, attn2d, conv2d, w_out_b, g2, wq_t, keys1, keys2, u_b, vt_b, tm, tr, tx, te):
    h, hn = _back(x2d, attn2d, conv2d, w_out_b, g2, tm)
    routing = _route(hn, wq_t, keys1, keys2, tr)
    return _experts(hn, u_b, vt_b, *routing, h, tx, te)


TILE_M = 512
TILE_ROUTE = 256
TILE_X = 512
TILE_E = 1024
CONV_CHUNK = 64


def kernel(x_prompt, x_sample, cache_k, cache_v, state_conv, page_table, norm1_g, w_in, q_norm_g, k_norm_g, conv_w, conv_b, conv_norm_g, w_out, norm2_g, peer_w_query, peer_keys1, peer_keys2, peer_u, peer_v):
    depth = w_in.shape[0]
    bsz, seq, d = x_prompt.shape
    db, dt, _ = x_sample.shape
    assert seq % TILE_M == 0 and TILE_M % MOBA_BLOCK == 0 and d - ATTN_W == ATTN_W
    assert (page_table.shape[1] * PAGE) % MOBA_BLOCK == 0, "partially filled own block is not supported"
    bd = _head_blockdiag()
    xp, xs = x_prompt, x_sample
    rows = {n: [] for n in ("kp", "vp", "cp", "ks", "vs", "cs")}
    for l in range(depth):
        w_in_b = w_in[l].astype(BF16)
        w_out_b = w_out[l].astype(BF16)
        wq_t = peer_w_query[l].T.astype(BF16)
        u_b = peer_u[l].astype(BF16)
        vt_b = peer_v[l].T.astype(BF16)
        g1 = norm1_g[l][None]
        g2 = norm2_g[l][None]
        qg = q_norm_g[l].reshape(1, ATTN_W)
        kg = k_norm_g[l].reshape(1, ATTN_W)
        cb = conv_b[l][None]
        cg = conv_norm_g[l][None]
        tail = functools.partial(_mixer_tail, w_out_b=w_out_b, g2=g2, wq_t=wq_t, keys1=peer_keys1[l],
                                 keys2=peer_keys2[l], u_b=u_b, vt_b=vt_b, te=TILE_E)

        x2d = xp.reshape(bsz * seq, d)
        q, k, v, u, kb, vb, km = _front(x2d, g1, w_in_b, qg, kg, bd, TILE_M, TILE_M // MOBA_BLOCK)
        r3 = lambda t: t.reshape(bsz, seq, ATTN_W)
        attn = _moba_prompt(r3(q), r3(kb), r3(vb), km.reshape(bsz, seq // MOBA_BLOCK, ATTN_W))
        conv = _conv(r3(u), jnp.zeros((bsz, HIST, ATTN_W), F32), conv_w[l], cb, cg, TILE_M, CONV_CHUNK)
        y = tail(x2d, attn.reshape(-1, ATTN_W), conv.reshape(-1, ATTN_W),
                 tm=TILE_M, tr=TILE_ROUTE, tx=TILE_X)
        xp = y.reshape(bsz, seq, d)
        rows["kp"].append(k.reshape(bsz, seq, HEADS, HEAD_DIM))
        rows["vp"].append(v.reshape(bsz, seq, HEADS, HEAD_DIM))
        rows["cp"].append(r3(u)[:, -(CONV_K - 1):])

        n_s = db * dt
        x2d = xs.reshape(n_s, d)
        q, k, v, u, _, _, _ = _front(x2d, g1, w_in_b, qg, kg, bd, n_s, 1)
        h4 = lambda t: t.reshape(db, dt, HEADS, HEAD_DIM)
        attn = _moba_sample(h4(q), h4(k), h4(v), cache_k, cache_v, page_table, l)
        u3 = u.reshape(db, dt, ATTN_W)
        hist = jnp.concatenate([jnp.zeros((db, HIST - (CONV_K - 1), ATTN_W), F32), state_conv[l]], axis=1)
        conv = _conv(u3, hist, conv_w[l], cb, cg, dt, dt)
        y = tail(x2d, attn.reshape(n_s, ATTN_W), conv.reshape(n_s, ATTN_W), tm=n_s, tr=n_s, tx=n_s)
        xs = y.reshape(db, dt, d)
        rows["ks"].append(h4(k))
        rows["vs"].append(h4(v))
        rows["cs"].append(jnp.concatenate([state_conv[l], u3], axis=1)[:, -(CONV_K - 1):])
    return (xp, xs, jnp.stack(rows["kp"]), jnp.stack(rows["vp"]), jnp.stack(rows["cp"]),
            jnp.stack(rows["ks"]), jnp.stack(rows["vs"]), jnp.stack(rows["cs"]))
```

```python
import functools

import jax
import jax.numpy as jnp
from jax import lax
from jax.experimental import pallas as pl
from jax.experimental.pallas import tpu as pltpu

F32 = jnp.float32
BF16 = jnp.bfloat16

HEADS = 8
HEAD_DIM = 64
ATTN_W = HEADS * HEAD_DIM
CONV_K = 31
MOBA_BLOCK = 256
MOBA_TOPK = 3
PAGE = 128
PEER_HEADS = 8
N_KEYS = 128
PEER_TOPK = 16
HALF_KEY = 128
EPS = 1e-6
NEG = float("-inf")

LANES = 128
BF16_ROWS = 16
ROUTE_UNROLL = 2
VMEM_LIMIT = 56 * 1024 * 1024


def _cparams(sem):
    return pltpu.CompilerParams(dimension_semantics=sem, vmem_limit_bytes=VMEM_LIMIT)


def _split_bf16(x):
    hi = x.astype(BF16)
    lo = (x - hi.astype(F32)).astype(BF16)
    return hi, lo


def _dot_nt(a, b):
    return lax.dot_general(a, b, (((1,), (1,)), ((), ())), preferred_element_type=F32)


def _dot_nt3(a, b):
    ah, al = _split_bf16(a)
    bh, bl = _split_bf16(b)
    return _dot_nt(ah, bh) + (_dot_nt(ah, bl) + _dot_nt(al, bh))


def _head_blockdiag():
    r = lax.broadcasted_iota(jnp.int32, (ATTN_W, ATTN_W), 0) // HEAD_DIM
    c = lax.broadcasted_iota(jnp.int32, (ATTN_W, ATTN_W), 1) // HEAD_DIM
    return jnp.where(r == c, 1.0 / HEAD_DIM, 0.0).astype(BF16)


def _front_body(x_ref, g1_ref, w_ref, qg_ref, kg_ref, bd_ref,
                q_ref, k_ref, v_ref, u_ref, kb_ref, vb_ref, km_ref, *, n_sub):
    x = x_ref[...]
    ms = jnp.mean(x * x, axis=-1, keepdims=True)
    xn = (x * lax.rsqrt(ms + EPS)) * g1_ref[...]
    proj = jnp.dot(xn.astype(BF16), w_ref[...], preferred_element_type=F32)
    bd = bd_ref[...]

    def head_norm(t, g):
        hi, lo = _split_bf16(t * t)
        msq = jnp.dot(hi, bd, preferred_element_type=F32) + jnp.dot(lo, bd, preferred_element_type=F32)
        return (t * lax.rsqrt(msq + EPS)) * g

    q_ref[...] = head_norm(proj[:, :ATTN_W], qg_ref[...])
    k = head_norm(proj[:, ATTN_W:2 * ATTN_W], kg_ref[...])
    v = proj[:, 2 * ATTN_W:3 * ATTN_W]
    k_ref[...] = k
    v_ref[...] = v
    kb_ref[...] = k.astype(BF16)
    vb_ref[...] = v.astype(BF16)
    a = proj[:, 3 * ATTN_W:3 * ATTN_W + ATTN_W]
    gt = proj[:, 4 * ATTN_W:]
    u_ref[...] = a * jax.nn.sigmoid(gt)
    rows = k.shape[0] // n_sub
    for s in range(n_sub):
        km_ref[0, s:s + 1, :] = jnp.sum(k[s * rows:(s + 1) * rows], axis=0, keepdims=True) * (1.0 / rows)


def _front(x2d, g1, w_in_b, qg, kg, bd, tm, n_sub):
    m, d = x2d.shape
    nw = w_in_b.shape[1]
    row = lambda i: (i, 0)
    fixed = lambda i: (0, 0)
    outs = (
        jax.ShapeDtypeStruct((m, ATTN_W), F32),
        jax.ShapeDtypeStruct((m, ATTN_W), F32),
        jax.ShapeDtypeStruct((m, ATTN_W), F32),
        jax.ShapeDtypeStruct((m, ATTN_W), F32),
        jax.ShapeDtypeStruct((m, ATTN_W), BF16),
        jax.ShapeDtypeStruct((m, ATTN_W), BF16),
        jax.ShapeDtypeStruct((m // tm, n_sub, ATTN_W), F32),
    )
    blk = pl.BlockSpec((tm, ATTN_W), row)
    return pl.pallas_call(
        functools.partial(_front_body, n_sub=n_sub),
        grid=(m // tm,),
        in_specs=[pl.BlockSpec((tm, d), row), pl.BlockSpec((1, d), fixed),
                  pl.BlockSpec((d, nw), fixed), pl.BlockSpec((1, ATTN_W), fixed),
                  pl.BlockSpec((1, ATTN_W), fixed), pl.BlockSpec((ATTN_W, ATTN_W), fixed)],
        out_specs=(blk, blk, blk, blk, blk, blk,
                   pl.BlockSpec((1, n_sub, ATTN_W), lambda i: (i, 0, 0))),
        out_shape=outs,
        compiler_params=_cparams(("arbitrary",)),
        name="front",
    )(x2d, g1, w_in_b, qg, kg, bd)


HIST = 32


def _conv_body(u_ref, h0_ref, w_ref, b_ref, g_ref, o_ref, buf_ref, *, ts, chunk):
    t = pl.program_id(1)

    @pl.when(t == 0)
    def _():
        buf_ref[0:HIST, :] = h0_ref[0]

    @pl.when(t != 0)
    def _():
        buf_ref[0:HIST, :] = buf_ref[ts:ts + HIST, :]

    buf_ref[HIST:HIST + ts, :] = u_ref[0]
    w = w_ref[...]
    off = HIST - (CONV_K - 1)
    for c in range(ts // chunk):
        acc = jnp.zeros((chunk, u_ref.shape[2]), F32)
        for j in range(CONV_K):
            acc = acc + buf_ref[c * chunk + off + j:c * chunk + off + j + chunk, :] * w[j:j + 1, :]
        y = acc + b_ref[...]
        ms = jnp.mean(y * y, axis=-1, keepdims=True)
        y = (y * lax.rsqrt(ms + EPS)) * g_ref[...]
        o_ref[0, c * chunk:(c + 1) * chunk, :] = y * jax.nn.sigmoid(y)


def _conv(u3, hist0, conv_w, conv_b, conv_g, ts, chunk):
    b, l, c = u3.shape
    fixed = lambda i, t: (0, 0)
    return pl.pallas_call(
        functools.partial(_conv_body, ts=ts, chunk=chunk),
        grid=(b, l // ts),
        in_specs=[pl.BlockSpec((1, ts, c), lambda i, t: (i, t, 0)),
                  pl.BlockSpec((1, HIST, c), lambda i, t: (i, 0, 0)),
                  pl.BlockSpec((CONV_K, c), fixed), pl.BlockSpec((1, c), fixed),
                  pl.BlockSpec((1, c), fixed)],
        out_specs=pl.BlockSpec((1, ts, c), lambda i, t: (i, t, 0)),
        out_shape=jax.ShapeDtypeStruct((b, l, c), F32),
        scratch_shapes=[pltpu.VMEM((ts + HIST, c), F32)],
        compiler_params=_cparams(("arbitrary", "arbitrary")),
        name="conv_branch",
    )(u3, hist0, conv_w, conv_b, conv_g)


def _block_select(gate, blk_idx, n_valid, nb):
    height = gate.shape[0]
    valid = blk_idx < n_valid
    g = jnp.where(valid, gate, NEG)
    rank = jnp.zeros(gate.shape, jnp.int32)
    for s in range(1, nb):
        fwd = pltpu.roll(g, height - s, 0)
        bwd = pltpu.roll(g, nb - s, 0)
        wraps = blk_idx + s >= nb
        other = jnp.where(wraps, bwd, fwd)
        beats = (other > g) | ((other == g) & wraps)
        rank = rank + beats.astype(jnp.int32)
    return valid & (rank < MOBA_TOPK)


MASK_BIAS = -1e30
ATTN_GROUP = 4


def _attn_body(q_ref, kb_ref, vb_ref, km_ref, ind_ref, o_ref, s_ref, mx_ref, l_ref, acc_ref, *, nb):
    i = pl.program_id(1)
    blk = MOBA_BLOCK
    hps = LANES // HEAD_DIM
    q = q_ref[0]
    km = km_ref[0]
    pad_rows = LANES - HEADS * nb
    kmt = jnp.concatenate([km] * HEADS + ([jnp.zeros((pad_rows, ATTN_W), F32)] if pad_rows else []), axis=0)
    r_head = lax.broadcasted_iota(jnp.int32, kmt.shape, 0) // nb
    c_head = lax.broadcasted_iota(jnp.int32, kmt.shape, 1) // HEAD_DIM
    kmt = jnp.where(r_head == c_head, kmt, 0.0)
    gate_t = _dot_nt3(kmt, q)
    rix = lax.broadcasted_iota(jnp.int32, (LANES, blk), 0)
    n_valid = jnp.where(rix < HEADS * nb, i, 0)
    sel_t = _block_select(gate_t, rix % nb, n_valid, nb)
    bias = jnp.where(sel_t, 0.0, MASK_BIAS).T
    lane = lax.broadcasted_iota(jnp.int32, (blk, LANES), 1)

    row = lax.broadcasted_iota(jnp.int32, (hps * blk, blk), 0) % blk
    col = lax.broadcasted_iota(jnp.int32, (hps * blk, blk), 1)
    causal = col <= row
    scale = HEAD_DIM ** -0.5
    own0 = pl.multiple_of(i * blk, blk)
    grp = ATTN_GROUP * blk
    n_grp = (i + ATTN_GROUP) // ATTN_GROUP

    def tilemax(s):
        out = s[:, :LANES]
        for t in range(1, s.shape[1] // LANES):
            out = jnp.maximum(out, s[:, t * LANES:(t + 1) * LANES])
        return out

    def tilesum(s):
        out = s[:, :LANES]
        for t in range(1, s.shape[1] // LANES):
            out = out + s[:, t * LANES:(t + 1) * LANES]
        return out

    for c in range(ATTN_W // LANES):
        slab = slice(c * LANES, (c + 1) * LANES)
        qs = q[:, slab] * scale
        stacked = []
        for half in range(hps):
            h = c * hps + half
            qm = jnp.where((lane // HEAD_DIM) == half, qs, 0.0).astype(BF16)
            bh = jnp.where((lane // nb) == h, bias, 0.0).astype(BF16)
            stacked.append(jnp.concatenate([qm, bh], axis=1))
        q2 = jnp.concatenate(stacked, axis=0)

        mx_ref[...] = jnp.full(mx_ref.shape, NEG, F32)

        def scores(g, carry, q2=q2, slab=slab):
            g0 = pl.multiple_of(g * grp, grp)
            kj = jnp.concatenate([kb_ref[0, pl.ds(g0, grp), slab], ind_ref[pl.ds(g0, grp), :]], axis=1)
            s = _dot_nt(q2, kj)
            s_ref[:, pl.ds(g0, grp)] = s
            mx_ref[...] = jnp.maximum(mx_ref[...], tilemax(s))
            return carry

        lax.fori_loop(0, n_grp, scores, 0)
        s = jnp.where(causal, _dot_nt(q2[:, :LANES], kb_ref[0, pl.ds(own0, blk), slab]), NEG)
        s_ref[:, pl.ds(own0, blk)] = s
        m = jnp.max(jnp.maximum(mx_ref[...], tilemax(s)), axis=-1, keepdims=True)
        l_ref[...] = jnp.zeros_like(l_ref)
        acc_ref[...] = jnp.zeros_like(acc_ref)

        def weigh(g, carry, m=m, slab=slab):
            g0 = pl.multiple_of(g * grp, grp)
            p = jnp.exp(s_ref[:, pl.ds(g0, grp)] - m)
            l_ref[...] += tilesum(p)
            acc_ref[...] += jnp.dot(p.astype(BF16), vb_ref[0, pl.ds(g0, grp), slab],
                                    preferred_element_type=F32)
            return carry

        lax.fori_loop(0, n_grp, weigh, 0)
        o = acc_ref[...] / jnp.sum(l_ref[...], axis=-1, keepdims=True)
        out = o[0:blk]
        for half in range(1, hps):
            out = jnp.where((lane // HEAD_DIM) == half, o[half * blk:(half + 1) * blk], out)
        o_ref[0, :, slab] = out


def _moba_prompt(q3, kb3, vb3, km3):
    b, s, w = q3.shape
    nb = s // MOBA_BLOCK
    assert HEADS * nb <= LANES and nb % ATTN_GROUP == 0
    rows = (LANES // HEAD_DIM) * MOBA_BLOCK
    key_blk = lax.broadcasted_iota(jnp.int32, (s, LANES), 0) // MOBA_BLOCK
    lane = lax.broadcasted_iota(jnp.int32, (s, LANES), 1)
    ind = ((key_blk == lane % nb) & (lane < HEADS * nb)).astype(BF16)
    return pl.pallas_call(
        functools.partial(_attn_body, nb=nb),
        grid=(b, nb),
        in_specs=[pl.BlockSpec((1, MOBA_BLOCK, w), lambda bi, i: (bi, i, 0)),
                  pl.BlockSpec((1, s, w), lambda bi, i: (bi, 0, 0)),
                  pl.BlockSpec((1, s, w), lambda bi, i: (bi, 0, 0)),
                  pl.BlockSpec((1, nb, w), lambda bi, i: (bi, 0, 0)),
                  pl.BlockSpec((s, LANES), lambda bi, i: (0, 0))],
        out_specs=pl.BlockSpec((1, MOBA_BLOCK, w), lambda bi, i: (bi, i, 0)),
        out_shape=jax.ShapeDtypeStruct((b, s, w), F32),
        scratch_shapes=[pltpu.VMEM((rows, s), F32), pltpu.VMEM((rows, LANES), F32),
                        pltpu.VMEM((rows, LANES), F32), pltpu.VMEM((rows, LANES), F32)],
        compiler_params=_cparams(("arbitrary", "arbitrary")),
        name="moba_prompt",
    )(q3, kb3, vb3, km3, ind)


def _back_body(x_ref, a_ref, c_ref, w_ref, g2_ref, h_ref, hn_ref):
    mix = jnp.concatenate([a_ref[...], c_ref[...]], axis=-1).astype(BF16)
    h = x_ref[...] + jnp.dot(mix, w_ref[...], preferred_element_type=F32)
    h_ref[...] = h
    ms = jnp.mean(h * h, axis=-1, keepdims=True)
    hn_ref[...] = ((h * lax.rsqrt(ms + EPS)) * g2_ref[...]).astype(BF16)


def _back(x2d, attn2d, conv2d, w_out_b, g2, tm):
    m, d = x2d.shape
    row = lambda i: (i, 0)
    fixed = lambda i: (0, 0)
    return pl.pallas_call(
        _back_body,
        grid=(m // tm,),
        in_specs=[pl.BlockSpec((tm, d), row), pl.BlockSpec((tm, ATTN_W), row),
                  pl.BlockSpec((tm, d - ATTN_W), row), pl.BlockSpec((d, d), fixed),
                  pl.BlockSpec((1, d), fixed)],
        out_specs=(pl.BlockSpec((tm, d), row), pl.BlockSpec((tm, d), row)),
        out_shape=(jax.ShapeDtypeStruct((m, d), F32), jax.ShapeDtypeStruct((m, d), BF16)),
        compiler_params=_cparams(("arbitrary",)),
        name="back",
    )(x2d, attn2d, conv2d, w_out_b, g2)


def _pair_candidates():
    return [(r1, r2) for r1 in range(PEER_TOPK) for r2 in range(PEER_TOPK)
            if (r1 + 1) * (r2 + 1) <= PEER_TOPK]


def _top_rounds(s, k, exact):
    n = s.shape[0]
    idx = lax.broadcasted_iota(jnp.int32, s.shape, 0)
    rank = jnp.full(s.shape, k, jnp.int32)
    vals = []
    for r in range(k):
        m = jnp.max(s, axis=0, keepdims=True)
        hit = s == m
        if exact:
            first = jnp.min(jnp.where(hit, idx, n), axis=0, keepdims=True)
            hit = idx == first
        rank = jnp.where(hit, r, rank)
        s = jnp.where(hit, NEG, s)
        vals.append(m)
    count = jnp.sum((rank < k).astype(F32), axis=0, keepdims=True)
    return vals, rank, count


def _route_body(hn_ref, wq_ref, k1_ref, k2_ref, e1w_ref, cnt_ref, e2_ref, rk2_ref, qh_ref):
    tm = hn_ref.shape[0]
    qh_ref[...] = _dot_nt(wq_ref[...], hn_ref[...])
    pairs = _pair_candidates()
    n_pad = (-len(pairs)) % 8

    def route_head(h, exact):
        base = pl.multiple_of(h * (2 * HALF_KEY), 2 * HALF_KEY)
        q1h, q1l = _split_bf16(qh_ref[pl.ds(base, HALF_KEY), :])
        q2h, q2l = _split_bf16(qh_ref[pl.ds(base + HALF_KEY, HALF_KEY), :])
        k1h, k1l = _split_bf16(k1_ref[h])
        k2h, k2l = _split_bf16(k2_ref[h])
        dot = functools.partial(jnp.dot, preferred_element_type=F32)
        s1 = dot(k1h, q1h) + (dot(k1h, q1l) + dot(k1l, q1h))
        s2 = dot(k2h, q2h) + (dot(k2h, q2l) + dot(k2l, q2h))
        v1, rk1, n1 = _top_rounds(s1, PEER_TOPK, exact)
        v2, rk2, n2 = _top_rounds(s2, PEER_TOPK, exact)
        cand = jnp.concatenate([v1[r1] + v2[r2] for r1, r2 in pairs]
                               + [jnp.full((n_pad, tm), NEG, F32)], axis=0)
        _, rkc, nc = _top_rounds(cand, PEER_TOPK, exact)
        picked = (rkc < PEER_TOPK).astype(F32)
        ez = jnp.exp(cand - cand[0:1, :]) * picked
        z = jnp.sum(ez, axis=0, keepdims=True)
        cnt = jnp.zeros((N_KEYS, tm), F32)
        for r1 in range(PEER_TOPK):
            rows = [i for i, (a, _) in enumerate(pairs) if a == r1]
            c_r1 = jnp.sum(picked[rows[0]:rows[-1] + 1, :], axis=0, keepdims=True)
            cnt = jnp.where(rk1 == r1, c_r1, cnt)
        e1w_ref[h] = jnp.exp(s1 - v1[0]) / z
        cnt_ref[h] = cnt
        e2_ref[h] = jnp.exp(s2 - v2[0]).astype(BF16)
        rk2_ref[h] = rk2.astype(F32).astype(BF16)
        return jnp.max(jnp.abs(n1 - PEER_TOPK) + jnp.abs(n2 - PEER_TOPK) + jnp.abs(nc - PEER_TOPK))

    def heads(i, carry):
        hs = [i * ROUTE_UNROLL + k for k in range(ROUTE_UNROLL)]
        n_tied = [route_head(h, exact=False) for h in hs]
        for h, n in zip(hs, n_tied):
            @pl.when(n > 0.0)
            def _(h=h):
                route_head(h, exact=True)

        return carry

    lax.fori_loop(0, PEER_HEADS // ROUTE_UNROLL, heads, 0)


def _route(hn, wq_t, keys1, keys2, tm):
    m, d = hn.shape
    nq = wq_t.shape[0]
    out = jax.ShapeDtypeStruct((PEER_HEADS, N_KEYS, m), F32)
    outb = jax.ShapeDtypeStruct((PEER_HEADS, N_KEYS, m), BF16)
    oblk = pl.BlockSpec((PEER_HEADS, N_KEYS, tm), lambda i: (0, 0, i))
    kblk = pl.BlockSpec((PEER_HEADS, N_KEYS, HALF_KEY), lambda i: (0, 0, 0))
    return pl.pallas_call(
        _route_body,
        grid=(m // tm,),
        in_specs=[pl.BlockSpec((tm, d), lambda i: (i, 0)), pl.BlockSpec((nq, d), lambda i: (0, 0)),
                  kblk, kblk],
        out_specs=(oblk, oblk, oblk, oblk),
        out_shape=(out, out, outb, outb),
        scratch_shapes=[pltpu.VMEM((nq, tm), F32)],
        compiler_params=_cparams(("arbitrary",)),
        name="peer_route",
    )(hn, wq_t, keys1, keys2)


def _experts_body(hn_ref, u_ref, vt_ref, e1w_ref, cnt_ref, e2_ref, rk2_ref, h_ref, y_ref, acc_ref, *, n_a):
    e = pl.program_id(1)

    @pl.when(e == 0)
    def _():
        acc_ref[...] = jnp.zeros_like(acc_ref)

    tm = hn_ref.shape[0]
    zero = jnp.zeros((), BF16)
    pre = _dot_nt(u_ref[...], hn_ref[...])
    act = (pre * 0.5 * (1.0 + lax.erf(pre * (2.0 ** -0.5)))).astype(BF16)
    parts = []
    for al in range(n_a):
        a = e * n_a + al
        c16 = [jnp.broadcast_to(cnt_ref[h, pl.ds(a, 1), :], (BF16_ROWS, tm)).astype(BF16)
               for h in range(PEER_HEADS)]
        w16 = [jnp.broadcast_to(e1w_ref[h, pl.ds(a, 1), :], (BF16_ROWS, tm)).astype(BF16)
               for h in range(PEER_HEADS)]
        for r in range(0, N_KEYS, BF16_ROWS):
            g = None
            for h in range(PEER_HEADS):
                t = jnp.where(rk2_ref[h, r:r + BF16_ROWS, :] < c16[h],
                              e2_ref[h, r:r + BF16_ROWS, :], zero) * w16[h]
                g = t if g is None else g + t
            parts.append(g * act[al * N_KEYS + r:al * N_KEYS + r + BF16_ROWS])
    p = jnp.concatenate(parts, axis=0)
    acc_ref[...] += jnp.dot(vt_ref[...], p, preferred_element_type=F32)

    @pl.when(e == pl.num_programs(1) - 1)
    def _():
        y_ref[...] = h_ref[...] + acc_ref[...].T


def _experts(hn, u_b, vt_b, e1w, cnt, e2, rk2, h, tm, te):
    m, d = hn.shape
    ne = u_b.shape[0]
    rblk = pl.BlockSpec((PEER_HEADS, N_KEYS, tm), lambda i, e: (0, 0, i))
    return pl.pallas_call(
        functools.partial(_experts_body, n_a=te // N_KEYS),
        grid=(m // tm, ne // te),
        in_specs=[pl.BlockSpec((tm, d), lambda i, e: (i, 0)),
                  pl.BlockSpec((te, d), lambda i, e: (e, 0)),
                  pl.BlockSpec((d, te), lambda i, e: (0, e)),
                  rblk, rblk, rblk, rblk,
                  pl.BlockSpec((tm, d), lambda i, e: (i, 0))],
        out_specs=pl.BlockSpec((tm, d), lambda i, e: (i, 0)),
        out_shape=jax.ShapeDtypeStruct((m, d), F32),
        scratch_shapes=[pltpu.VMEM((d, tm), F32)],
        compiler_params=_cparams(("arbitrary", "arbitrary")),
        name="peer_experts",
    )(hn, u_b, vt_b, e1w, cnt, e2, rk2, h)


PAGES_PER_BLOCK = MOBA_BLOCK // PAGE
SAMPLE_GROUP = 8


def _dot_nn3(a, b):
    ah, al = _split_bf16(a)
    bh, bl = _split_bf16(b)
    dot = functools.partial(jnp.dot, preferred_element_type=F32)
    return dot(ah, bh) + (dot(ah, bl) + dot(al, bh))


def _sample_body(pt_ref, q_ref, kn_ref, vn_ref, *refs, nbp):
    n_pg = SAMPLE_GROUP * PAGES_PER_BLOCK
    k_pages = refs[:n_pg]
    v_pages = refs[n_pg:2 * n_pg]
    o_ref, km_s, m_s, acc_s = refs[2 * n_pg:]
    j = pl.program_id(1)
    dt = q_ref.shape[2]
    grp = SAMPLE_GROUP
    keys = grp * MOBA_BLOCK
    scale = HEAD_DIM ** -0.5
    lane_d = lax.broadcasted_iota(jnp.int32, (HEAD_DIM, LANES), 1)
    lane_t = lax.broadcasted_iota(jnp.int32, (dt, LANES), 1)
    own_blk = (lax.broadcasted_iota(jnp.int32, (grp, keys), 1) // MOBA_BLOCK
               == lax.broadcasted_iota(jnp.int32, (grp, keys), 0))
    ones = jnp.ones((HEAD_DIM, keys), BF16)

    @pl.when(j == 0)
    def _():
        km_s[...] = jnp.zeros_like(km_s)
        m_s[...] = jnp.zeros_like(m_s)

    for h in range(HEADS):
        kt = jnp.concatenate([r[h] for r in k_pages], axis=1)
        vt = jnp.concatenate([r[h] for r in v_pages], axis=1)
        qs = (q_ref[0, h] * scale).astype(BF16)
        s = jnp.dot(qs, kt.astype(BF16), preferred_element_type=F32)
        m = jnp.max(s, axis=-1, keepdims=True)
        p = jnp.exp(s - m)
        p_rows = jnp.concatenate(
            [jnp.where(own_blk, jnp.broadcast_to(p[t:t + 1], (grp, keys)), 0.0) for t in range(dt)],
            axis=0).astype(BF16)
        acc = _dot_nt(p_rows, jnp.concatenate([vt.astype(BF16), ones], axis=0))
        g0 = pl.multiple_of(j * grp, grp)
        for t in range(dt):
            acc_s[h, t, pl.ds(g0, grp), :] = acc[t * grp:(t + 1) * grp]
        m_s[h] = jnp.where(lane_t // grp == j, m, m_s[h])
        km = km_s[h]
        for g in range(grp):
            kmean = jnp.sum(kt[:, g * MOBA_BLOCK:(g + 1) * MOBA_BLOCK], axis=1, keepdims=True)
            km = jnp.where(lane_d == j * grp + g, kmean * (1.0 / MOBA_BLOCK), km)
        km_s[h] = km

    @pl.when(j == pl.num_programs(1) - 1)
    def _():
        tri = (lax.broadcasted_iota(jnp.int32, (dt, dt), 1) <= lax.broadcasted_iota(jnp.int32, (dt, dt), 0))
        own_tok = lax.broadcasted_iota(jnp.int32, (dt, dt * nbp), 1) // nbp == lax.broadcasted_iota(
            jnp.int32, (dt, dt * nbp), 0)
        for h in range(HEADS):
            q = q_ref[0, h]
            gate = jnp.where(lane_t < nbp, _dot_nn3(q, km_s[h]), NEG)
            picked = jnp.zeros(gate.shape, jnp.bool_)
            for _ in range(MOBA_TOPK):
                top = jnp.max(gate, axis=-1, keepdims=True)
                first = jnp.min(jnp.where(gate == top, lane_t, LANES), axis=-1, keepdims=True)
                hit = lane_t == first
                picked = picked | hit
                gate = jnp.where(hit, NEG, gate)
            qs = (q * scale).astype(BF16)
            sn = jnp.where(tri, _dot_nt(qs, kn_ref[0, h].astype(BF16)), NEG)
            mn = jnp.max(sn, axis=-1, keepdims=True)
            pn = jnp.exp(sn - mn)
            ln = jnp.sum(pn, axis=-1, keepdims=True)
            an = jnp.dot(pn.astype(BF16), vn_ref[0, h].astype(BF16), preferred_element_type=F32)
            m_all = m_s[h]
            m_tot = jnp.maximum(jnp.max(jnp.where(picked, m_all, NEG), axis=-1, keepdims=True), mn)
            w = jnp.where(picked, jnp.exp(m_all - m_tot), 0.0)
            wn = jnp.exp(mn - m_tot)
            w_rows = jnp.where(own_tok, jnp.concatenate([w[:, :nbp]] * dt, axis=1), 0.0)
            past = _dot_nn3(w_rows, acc_s[h].reshape(dt * nbp, 2 * HEAD_DIM))
            den = past[:, HEAD_DIM:HEAD_DIM + 1] + wn * ln
            o_ref[0, h] = (past[:, :HEAD_DIM] + wn * an) / den


def _moba_sample(q4, kn4, vn4, cache_k, cache_v, page_table, layer):
    db, _, dt, _ = q4.shape
    nbp = (page_table.shape[1] * PAGE) // MOBA_BLOCK
    assert MOBA_TOPK <= nbp <= LANES and nbp % SAMPLE_GROUP == 0
    kt5 = jnp.transpose(cache_k, (0, 1, 3, 4, 2))
    vt5 = jnp.transpose(cache_v, (0, 1, 3, 4, 2))
    new_blk = pl.BlockSpec((1, HEADS, dt, HEAD_DIM), lambda b, j, pt: (b, 0, 0, 0))
    n_pg = SAMPLE_GROUP * PAGES_PER_BLOCK

    def page_spec(r):
        return pl.BlockSpec((None, None, HEADS, HEAD_DIM, PAGE),
                            lambda b, j, pt: (layer, pt[b, n_pg * j + r], 0, 0, 0))

    pages = [page_spec(r) for r in range(n_pg)]
    return pl.pallas_call(
        functools.partial(_sample_body, nbp=nbp),
        grid_spec=pltpu.PrefetchScalarGridSpec(
            num_scalar_prefetch=1,
            grid=(db, nbp // SAMPLE_GROUP),
            in_specs=[new_blk, new_blk, new_blk] + pages + pages,
            out_specs=new_blk,
            scratch_shapes=[pltpu.VMEM((HEADS, HEAD_DIM, LANES), F32), pltpu.VMEM((HEADS, dt, LANES), F32),
                            pltpu.VMEM((HEADS, dt, nbp, 2 * HEAD_DIM), F32)]),
        out_shape=jax.ShapeDtypeStruct((db, HEADS, dt, HEAD_DIM), F32),
        compiler_params=_cparams(("arbitrary", "arbitrary")),
        name="moba_sample",
    )(page_table, q4, kn4, vn4, *([kt5] * n_pg), *([vt5] * n_pg))


def _mixer_tail(x2d, attn2d, conv2d, w_out_b, g2, wq_t, keys1, keys2, u_b, vt_b, tm, tr, tx, te):
    h, hn = _back(x2d, attn2d, conv2d, w_out_b, g2, tm)
    routing = _route(hn, wq_t, keys1, keys2, tr)
    return _experts(hn, u_b, vt_b, *routing, h, tx, te)


TILE_M = 512
TILE_ROUTE = 256
TILE_X = 512
TILE_E = 1024
CONV_CHUNK = 64


def kernel(x_prompt, x_sample, cache_k, cache_v, state_conv, page_table, norm1_g, w_in, q_norm_g, k_norm_g, conv_w, conv_b, conv_norm_g, w_out, norm2_g, peer_w_query, peer_keys1, peer_keys2, peer_u, peer_v):
    depth = w_in.shape[0]
    bsz, seq, d = x_prompt.shape
    db, dt, _ = x_sample.shape
    assert seq % TILE_M == 0 and TILE_M % MOBA_BLOCK == 0 and d - ATTN_W == ATTN_W
    assert (page_table.shape[1] * PAGE) % MOBA_BLOCK == 0, "partially filled own block is not supported"
    bd = _head_blockdiag()
    xp, xs = x_prompt, x_sample
    rows = {n: [] for n in ("kp", "vp", "cp", "ks", "vs", "cs")}
    for l in range(depth):
        w_in_b = w_in[l].astype(BF16)
        w_out_b = w_out[l].astype(BF16)
        wq_t = peer_w_query[l].T.astype(BF16)
        u_b = peer_u[l].astype(BF16)
        vt_b = peer_v[l].T.astype(BF16)
        g1 = norm1_g[l][None]
        g2 = norm2_g[l][None]
        qg = q_norm_g[l].reshape(1, ATTN_W)
        kg = k_norm_g[l].reshape(1, ATTN_W)
        cb = conv_b[l][None]
        cg = conv_norm_g[l][None]
        tail = functools.partial(_mixer_tail, w_out_b=w_out_b, g2=g2, wq_t=wq_t, keys1=peer_keys1[l],
                                 keys2=peer_keys2[l], u_b=u_b, vt_b=vt_b, te=TILE_E)

        x2d = xp.reshape(bsz * seq, d)
        q, k, v, u, kb, vb, km = _front(x2d, g1, w_in_b, qg, kg, bd, TILE_M, TILE_M // MOBA_BLOCK)
        r3 = lambda t: t.reshape(bsz, seq, ATTN_W)
        attn = _moba_prompt(r3(q), r3(kb), r3(vb), km.reshape(bsz, seq // MOBA_BLOCK, ATTN_W))
        conv = _conv(r3(u), jnp.zeros((bsz, HIST, ATTN_W), F32), conv_w[l], cb, cg, TILE_M, CONV_CHUNK)
        y = tail(x2d, attn.reshape(-1, ATTN_W), conv.reshape(-1, ATTN_W),
                 tm=TILE_M, tr=TILE_ROUTE, tx=TILE_X)
        xp = y.reshape(bsz, seq, d)
        rows["kp"].append(k.reshape(bsz, seq, HEADS, HEAD_DIM))
        rows["vp"].append(v.reshape(bsz, seq, HEADS, HEAD_DIM))
        rows["cp"].append(r3(u)[:, -(CONV_K - 1):])

        n_s = db * dt
        x2d = xs.reshape(n_s, d)
        q, k, v, u, _, _, _ = _front(x2d, g1, w_in_b, qg, kg, bd, n_s, 1)
        h4 = lambda t: t.reshape(db, dt, HEADS, HEAD_DIM)
        hm = lambda t: h4(t).transpose(0, 2, 1, 3)
        attn = _moba_sample(hm(q), hm(k), hm(v), cache_k, cache_v, page_table, l).transpose(0, 2, 1, 3)
        u3 = u.reshape(db, dt, ATTN_W)
        hist = jnp.concatenate([jnp.zeros((db, HIST - (CONV_K - 1), ATTN_W), F32), state_conv[l]], axis=1)
        conv = _conv(u3, hist, conv_w[l], cb, cg, dt, dt)
        y = tail(x2d, attn.reshape(n_s, ATTN_W), conv.reshape(n_s, ATTN_W), tm=n_s, tr=n_s, tx=n_s)
        xs = y.reshape(db, dt, d)
        rows["ks"].append(h4(k))
        rows["vs"].append(h4(v))
        rows["cs"].append(jnp.concatenate([state_conv[l], u3], axis=1)[:, -(CONV_K - 1):])
    return (xp, xs, jnp.stack(rows["kp"]), jnp.stack(rows["vp"]), jnp.stack(rows["cp"]),
            jnp.stack(rows["ks"]), jnp.stack(rows["vs"]), jnp.stack(rows["cs"]))
```

```python
import functools

import jax
import jax.numpy as jnp
from jax import lax
from jax.experimental import pallas as pl
from jax.experimental.pallas import tpu as pltpu

F32 = jnp.float32
BF16 = jnp.bfloat16

HEADS = 8
HEAD_DIM = 64
ATTN_W = HEADS * HEAD_DIM
CONV_K = 31
MOBA_BLOCK = 256
MOBA_TOPK = 3
PAGE = 128
PEER_HEADS = 8
N_KEYS = 128
PEER_TOPK = 16
HALF_KEY = 128
EPS = 1e-6
NEG = float("-inf")

LANES = 128
BF16_ROWS = 16
ROUTE_UNROLL = 2
WEIGHT_HOP = 4
VMEM_LIMIT = 56 * 1024 * 1024


def _cparams(sem):
    return pltpu.CompilerParams(dimension_semantics=sem, vmem_limit_bytes=VMEM_LIMIT)


def _split_bf16(x):
    hi = x.astype(BF16)
    lo = (x - hi.astype(F32)).astype(BF16)
    return hi, lo


def _dot_nt(a, b):
    return lax.dot_general(a, b, (((1,), (1,)), ((), ())), preferred_element_type=F32)


def _dot_nt3(a, b):
    ah, al = _split_bf16(a)
    bh, bl = _split_bf16(b)
    return _dot_nt(ah, bh) + (_dot_nt(ah, bl) + _dot_nt(al, bh))


def _head_blockdiag():
    r = lax.broadcasted_iota(jnp.int32, (ATTN_W, ATTN_W), 0) // HEAD_DIM
    c = lax.broadcasted_iota(jnp.int32, (ATTN_W, ATTN_W), 1) // HEAD_DIM
    return jnp.where(r == c, 1.0 / HEAD_DIM, 0.0).astype(BF16)


def _front_body(x_ref, g1_ref, w_ref, qg_ref, kg_ref, bd_ref,
                q_ref, k_ref, v_ref, u_ref, kb_ref, vb_ref, km_ref, *, n_sub):
    x = x_ref[...]
    ms = jnp.mean(x * x, axis=-1, keepdims=True)
    xn = (x * lax.rsqrt(ms + EPS)) * g1_ref[...]
    proj = jnp.dot(xn.astype(BF16), w_ref[...], preferred_element_type=F32)
    bd = bd_ref[...]

    def head_norm(t, g):
        hi, lo = _split_bf16(t * t)
        msq = jnp.dot(hi, bd, preferred_element_type=F32) + jnp.dot(lo, bd, preferred_element_type=F32)
        return (t * lax.rsqrt(msq + EPS)) * g

    q_ref[...] = head_norm(proj[:, :ATTN_W], qg_ref[...])
    k = head_norm(proj[:, ATTN_W:2 * ATTN_W], kg_ref[...])
    v = proj[:, 2 * ATTN_W:3 * ATTN_W]
    k_ref[...] = k
    v_ref[...] = v
    kb_ref[...] = k.astype(BF16)
    vb_ref[...] = v.astype(BF16)
    a = proj[:, 3 * ATTN_W:3 * ATTN_W + ATTN_W]
    gt = proj[:, 4 * ATTN_W:]
    u_ref[...] = a * jax.nn.sigmoid(gt)
    rows = k.shape[0] // n_sub
    for s in range(n_sub):
        km_ref[0, s:s + 1, :] = jnp.sum(k[s * rows:(s + 1) * rows], axis=0, keepdims=True) * (1.0 / rows)


def _front(x2d, g1, w_in_b, qg, kg, bd, tm, n_sub):
    m, d = x2d.shape
    nw = w_in_b.shape[1]
    row = lambda i: (i, 0)
    fixed = lambda i: (0, 0)
    outs = (
        jax.ShapeDtypeStruct((m, ATTN_W), F32),
        jax.ShapeDtypeStruct((m, ATTN_W), F32),
        jax.ShapeDtypeStruct((m, ATTN_W), F32),
        jax.ShapeDtypeStruct((m, ATTN_W), F32),
        jax.ShapeDtypeStruct((m, ATTN_W), BF16),
        jax.ShapeDtypeStruct((m, ATTN_W), BF16),
        jax.ShapeDtypeStruct((m // tm, n_sub, ATTN_W), F32),
    )
    blk = pl.BlockSpec((tm, ATTN_W), row)
    return pl.pallas_call(
        functools.partial(_front_body, n_sub=n_sub),
        grid=(m // tm,),
        in_specs=[pl.BlockSpec((tm, d), row), pl.BlockSpec((1, d), fixed),
                  pl.BlockSpec((d, nw), fixed), pl.BlockSpec((1, ATTN_W), fixed),
                  pl.BlockSpec((1, ATTN_W), fixed), pl.BlockSpec((ATTN_W, ATTN_W), fixed)],
        out_specs=(blk, blk, blk, blk, blk, blk,
                   pl.BlockSpec((1, n_sub, ATTN_W), lambda i: (i, 0, 0))),
        out_shape=outs,
        compiler_params=_cparams(("arbitrary",)),
        name="front",
    )(x2d, g1, w_in_b, qg, kg, bd)


HIST = 32


def _conv_body(u_ref, h0_ref, w_ref, b_ref, g_ref, o_ref, buf_ref, *, ts, chunk):
    t = pl.program_id(1)

    @pl.when(t == 0)
    def _():
        buf_ref[0:HIST, :] = h0_ref[0]

    @pl.when(t != 0)
    def _():
        buf_ref[0:HIST, :] = buf_ref[ts:ts + HIST, :]

    buf_ref[HIST:HIST + ts, :] = u_ref[0]
    w = w_ref[...]
    off = HIST - (CONV_K - 1)
    for c in range(ts // chunk):
        acc = jnp.zeros((chunk, u_ref.shape[2]), F32)
        for j in range(CONV_K):
            acc = acc + buf_ref[c * chunk + off + j:c * chunk + off + j + chunk, :] * w[j:j + 1, :]
        y = acc + b_ref[...]
        ms = jnp.mean(y * y, axis=-1, keepdims=True)
        y = (y * lax.rsqrt(ms + EPS)) * g_ref[...]
        o_ref[0, c * chunk:(c + 1) * chunk, :] = y * jax.nn.sigmoid(y)


def _conv(u3, hist0, conv_w, conv_b, conv_g, ts, chunk):
    b, l, c = u3.shape
    fixed = lambda i, t: (0, 0)
    return pl.pallas_call(
        functools.partial(_conv_body, ts=ts, chunk=chunk),
        grid=(b, l // ts),
        in_specs=[pl.BlockSpec((1, ts, c), lambda i, t: (i, t, 0)),
                  pl.BlockSpec((1, HIST, c), lambda i, t: (i, 0, 0)),
                  pl.BlockSpec((CONV_K, c), fixed), pl.BlockSpec((1, c), fixed),
                  pl.BlockSpec((1, c), fixed)],
        out_specs=pl.BlockSpec((1, ts, c), lambda i, t: (i, t, 0)),
        out_shape=jax.ShapeDtypeStruct((b, l, c), F32),
        scratch_shapes=[pltpu.VMEM((ts + HIST, c), F32)],
        compiler_params=_cparams(("arbitrary", "arbitrary")),
        name="conv_branch",
    )(u3, hist0, conv_w, conv_b, conv_g)


def _block_select(gate, blk_idx, n_valid, nb):
    height = gate.shape[0]
    valid = blk_idx < n_valid
    g = jnp.where(valid, gate, NEG)
    rank = jnp.zeros(gate.shape, jnp.int32)
    for s in range(1, nb):
        fwd = pltpu.roll(g, height - s, 0)
        bwd = pltpu.roll(g, nb - s, 0)
        wraps = blk_idx + s >= nb
        other = jnp.where(wraps, bwd, fwd)
        beats = (other > g) | ((other == g) & wraps)
        rank = rank + beats.astype(jnp.int32)
    return valid & (rank < MOBA_TOPK)


MASK_BIAS = -1e30
ATTN_GROUP = 4


def _attn_body(q_ref, kb_ref, vb_ref, km_ref, ind_ref, o_ref, s_ref, mx_ref, l_ref, acc_ref, *, nb):
    i = pl.program_id(1)
    blk = MOBA_BLOCK
    hps = LANES // HEAD_DIM
    q = q_ref[0]
    km = km_ref[0]
    pad_rows = LANES - HEADS * nb
    kmt = jnp.concatenate([km] * HEADS + ([jnp.zeros((pad_rows, ATTN_W), F32)] if pad_rows else []), axis=0)
    r_head = lax.broadcasted_iota(jnp.int32, kmt.shape, 0) // nb
    c_head = lax.broadcasted_iota(jnp.int32, kmt.shape, 1) // HEAD_DIM
    kmt = jnp.where(r_head == c_head, kmt, 0.0)
    gate_t = _dot_nt3(kmt, q)
    rix = lax.broadcasted_iota(jnp.int32, (LANES, blk), 0)
    n_valid = jnp.where(rix < HEADS * nb, i, 0)
    sel_t = _block_select(gate_t, rix % nb, n_valid, nb)
    bias = jnp.where(sel_t, 0.0, MASK_BIAS).T
    lane = lax.broadcasted_iota(jnp.int32, (blk, LANES), 1)

    row = lax.broadcasted_iota(jnp.int32, (hps * blk, blk), 0) % blk
    col = lax.broadcasted_iota(jnp.int32, (hps * blk, blk), 1)
    causal = col <= row
    scale = HEAD_DIM ** -0.5
    own0 = pl.multiple_of(i * blk, blk)
    grp = ATTN_GROUP * blk
    n_grp = (i + ATTN_GROUP) // ATTN_GROUP

    def tilemax(s):
        out = s[:, :LANES]
        for t in range(1, s.shape[1] // LANES):
            out = jnp.maximum(out, s[:, t * LANES:(t + 1) * LANES])
        return out

    def tilesum(s):
        out = s[:, :LANES]
        for t in range(1, s.shape[1] // LANES):
            out = out + s[:, t * LANES:(t + 1) * LANES]
        return out

    for c in range(ATTN_W // LANES):
        slab = slice(c * LANES, (c + 1) * LANES)
        qs = q[:, slab] * scale
        stacked = []
        for half in range(hps):
            h = c * hps + half
            qm = jnp.where((lane // HEAD_DIM) == half, qs, 0.0).astype(BF16)
            bh = jnp.where((lane // nb) == h, bias, 0.0).astype(BF16)
            stacked.append(jnp.concatenate([qm, bh], axis=1))
        q2 = jnp.concatenate(stacked, axis=0)

        mx_ref[...] = jnp.full(mx_ref.shape, NEG, F32)

        def scores(g, carry, q2=q2, slab=slab):
            g0 = pl.multiple_of(g * grp, grp)
            kj = jnp.concatenate([kb_ref[0, pl.ds(g0, grp), slab], ind_ref[pl.ds(g0, grp), :]], axis=1)
            s = _dot_nt(q2, kj)
            s_ref[:, pl.ds(g0, grp)] = s
            mx_ref[...] = jnp.maximum(mx_ref[...], tilemax(s))
            return carry

        lax.fori_loop(0, n_grp, scores, 0)
        s = jnp.where(causal, _dot_nt(q2[:, :LANES], kb_ref[0, pl.ds(own0, blk), slab]), NEG)
        s_ref[:, pl.ds(own0, blk)] = s
        m = jnp.max(jnp.maximum(mx_ref[...], tilemax(s)), axis=-1, keepdims=True)
        l_ref[...] = jnp.zeros_like(l_ref)
        acc_ref[...] = jnp.zeros_like(acc_ref)

        def weigh(g, carry, m=m, slab=slab):
            g0 = pl.multiple_of(g * grp, grp)
            p = jnp.exp(s_ref[:, pl.ds(g0, grp)] - m)
            l_ref[...] += tilesum(p)
            acc_ref[...] += jnp.dot(p.astype(BF16), vb_ref[0, pl.ds(g0, grp), slab],
                                    preferred_element_type=F32)
            return carry

        lax.fori_loop(0, n_grp, weigh, 0)
        o = acc_ref[...] / jnp.sum(l_ref[...], axis=-1, keepdims=True)
        out = o[0:blk]
        for half in range(1, hps):
            out = jnp.where((lane // HEAD_DIM) == half, o[half * blk:(half + 1) * blk], out)
        o_ref[0, :, slab] = out


def _moba_prompt(q3, kb3, vb3, km3):
    b, s, w = q3.shape
    nb = s // MOBA_BLOCK
    assert HEADS * nb <= LANES and nb % ATTN_GROUP == 0
    rows = (LANES // HEAD_DIM) * MOBA_BLOCK
    key_blk = lax.broadcasted_iota(jnp.int32, (s, LANES), 0) // MOBA_BLOCK
    lane = lax.broadcasted_iota(jnp.int32, (s, LANES), 1)
    ind = ((key_blk == lane % nb) & (lane < HEADS * nb)).astype(BF16)
    return pl.pallas_call(
        functools.partial(_attn_body, nb=nb),
        grid=(b, nb),
        in_specs=[pl.BlockSpec((1, MOBA_BLOCK, w), lambda bi, i: (bi, i, 0)),
                  pl.BlockSpec((1, s, w), lambda bi, i: (bi, 0, 0)),
                  pl.BlockSpec((1, s, w), lambda bi, i: (bi, 0, 0)),
                  pl.BlockSpec((1, nb, w), lambda bi, i: (bi, 0, 0)),
                  pl.BlockSpec((s, LANES), lambda bi, i: (0, 0))],
        out_specs=pl.BlockSpec((1, MOBA_BLOCK, w), lambda bi, i: (bi, i, 0)),
        out_shape=jax.ShapeDtypeStruct((b, s, w), F32),
        scratch_shapes=[pltpu.VMEM((rows, s), F32), pltpu.VMEM((rows, LANES), F32),
                        pltpu.VMEM((rows, LANES), F32), pltpu.VMEM((rows, LANES), F32)],
        compiler_params=_cparams(("arbitrary", "arbitrary")),
        name="moba_prompt",
    )(q3, kb3, vb3, km3, ind)


def _back_body(x_ref, a_ref, c_ref, w_ref, g2_ref, h_ref, hn_ref):
    mix = jnp.concatenate([a_ref[...], c_ref[...]], axis=-1).astype(BF16)
    h = x_ref[...] + jnp.dot(mix, w_ref[...], preferred_element_type=F32)
    h_ref[...] = h
    ms = jnp.mean(h * h, axis=-1, keepdims=True)
    hn_ref[...] = ((h * lax.rsqrt(ms + EPS)) * g2_ref[...]).astype(BF16)


def _back(x2d, attn2d, conv2d, w_out_b, g2, tm):
    m, d = x2d.shape
    row = lambda i: (i, 0)
    fixed = lambda i: (0, 0)
    return pl.pallas_call(
        _back_body,
        grid=(m // tm,),
        in_specs=[pl.BlockSpec((tm, d), row), pl.BlockSpec((tm, ATTN_W), row),
                  pl.BlockSpec((tm, d - ATTN_W), row), pl.BlockSpec((d, d), fixed),
                  pl.BlockSpec((1, d), fixed)],
        out_specs=(pl.BlockSpec((tm, d), row), pl.BlockSpec((tm, d), row)),
        out_shape=(jax.ShapeDtypeStruct((m, d), F32), jax.ShapeDtypeStruct((m, d), BF16)),
        compiler_params=_cparams(("arbitrary",)),
        name="back",
    )(x2d, attn2d, conv2d, w_out_b, g2)


def _pair_candidates():
    return [(r1, r2) for r1 in range(PEER_TOPK) for r2 in range(PEER_TOPK)
            if (r1 + 1) * (r2 + 1) <= PEER_TOPK]


RANK_MARK = 2.0 ** 100


def _top_rounds(s, k, exact):
    n = s.shape[0]
    idx = lax.broadcasted_iota(jnp.int32, s.shape, 0)
    vals = []
    for r in range(k):
        m = jnp.max(s, axis=0, keepdims=True)
        hit = s == m
        if exact:
            first = jnp.min(jnp.where(hit, idx, n), axis=0, keepdims=True)
            hit = idx == first
        s = jnp.where(hit, -(r + 1.0) * RANK_MARK, s)
        vals.append(m)
    rank = jnp.where(s <= -RANK_MARK, s * (-1.0 / RANK_MARK) - 1.0, float(k))
    count = jnp.sum((rank < k).astype(F32), axis=0, keepdims=True)
    return vals, rank, count


def _route_body(hn_ref, wq_ref, k1_ref, k2_ref, e1w_ref, cnt_ref, e2_ref, rk2_ref, qh_ref):
    tm = hn_ref.shape[0]
    qh_ref[...] = _dot_nt(wq_ref[...], hn_ref[...])
    pairs = _pair_candidates()
    n_pad = (-len(pairs)) % 8

    def route_head(h, exact):
        base = pl.multiple_of(h * (2 * HALF_KEY), 2 * HALF_KEY)
        q1h, q1l = _split_bf16(qh_ref[pl.ds(base, HALF_KEY), :])
        q2h, q2l = _split_bf16(qh_ref[pl.ds(base + HALF_KEY, HALF_KEY), :])
        k1h, k1l = _split_bf16(k1_ref[h])
        k2h, k2l = _split_bf16(k2_ref[h])
        dot = functools.partial(jnp.dot, preferred_element_type=F32)
        s1 = dot(k1h, q1h) + (dot(k1h, q1l) + dot(k1l, q1h))
        s2 = dot(k2h, q2h) + (dot(k2h, q2l) + dot(k2l, q2h))
        v1, rk1, n1 = _top_rounds(s1, PEER_TOPK, exact)
        v2, rk2, n2 = _top_rounds(s2, PEER_TOPK, exact)
        cand = jnp.concatenate([v1[r1] + v2[r2] for r1, r2 in pairs]
                               + [jnp.full((n_pad, tm), NEG, F32)], axis=0)
        _, rkc, nc = _top_rounds(cand, PEER_TOPK, exact)
        picked = (rkc < PEER_TOPK).astype(F32)
        ez = jnp.exp(cand - cand[0:1, :]) * picked
        z = jnp.sum(ez, axis=0, keepdims=True)
        cnt = jnp.zeros((N_KEYS, tm), F32)
        for r1 in range(PEER_TOPK):
            rows = [i for i, (a, _) in enumerate(pairs) if a == r1]
            c_r1 = jnp.sum(picked[rows[0]:rows[-1] + 1, :], axis=0, keepdims=True)
            cnt = jnp.where(rk1 == r1, c_r1, cnt)
        e1w_ref[h] = jnp.exp(s1 - v1[0]) / z
        cnt_ref[h] = cnt
        e2_ref[h] = jnp.exp(s2 - v2[0]).astype(BF16)
        rk2_ref[h] = rk2.astype(F32).astype(BF16)
        return jnp.max(jnp.abs(n1 - PEER_TOPK) + jnp.abs(n2 - PEER_TOPK) + jnp.abs(nc - PEER_TOPK))

    def heads(i, carry):
        hs = [i * ROUTE_UNROLL + k for k in range(ROUTE_UNROLL)]
        n_tied = [route_head(h, exact=False) for h in hs]
        for h, n in zip(hs, n_tied):
            @pl.when(n > 0.0)
            def _(h=h):
                route_head(h, exact=True)

        return carry

    lax.fori_loop(0, PEER_HEADS // ROUTE_UNROLL, heads, 0)


def _route(hn, wq_t, keys1, keys2, tm):
    m, d = hn.shape
    nq = wq_t.shape[0]
    out = jax.ShapeDtypeStruct((PEER_HEADS, N_KEYS, m), F32)
    outb = jax.ShapeDtypeStruct((PEER_HEADS, N_KEYS, m), BF16)
    oblk = pl.BlockSpec((PEER_HEADS, N_KEYS, tm), lambda i: (0, 0, i))
    kblk = pl.BlockSpec((PEER_HEADS, N_KEYS, HALF_KEY), lambda i: (0, 0, 0))
    return pl.pallas_call(
        _route_body,
        grid=(m // tm,),
        in_specs=[pl.BlockSpec((tm, d), lambda i: (i, 0)), pl.BlockSpec((nq, d), lambda i: (0, 0)),
                  kblk, kblk],
        out_specs=(oblk, oblk, oblk, oblk),
        out_shape=(out, out, outb, outb),
        scratch_shapes=[pltpu.VMEM((nq, tm), F32)],
        compiler_params=_cparams(("arbitrary",)),
        name="peer_route",
    )(hn, wq_t, keys1, keys2)


def _experts_body(hn_ref, u_ref, vt_ref, e1w_ref, cnt_ref, e2_ref, rk2_ref, h_ref, y_ref, acc_ref, g_ref, *, n_a):
    e = pl.program_id(1)
    n_e = pl.num_programs(1)
    tm = hn_ref.shape[0]
    half_a = n_a // 2

    def build(tile, key1s, chained=True):
        zero = jnp.zeros((), BF16)
        tiles = []
        for al in key1s:
            a = tile * n_a + al
            c16 = [jnp.broadcast_to(cnt_ref[h, pl.ds(a, 1), :], (BF16_ROWS, tm)).astype(BF16)
                   for h in range(PEER_HEADS)]
            w16 = [jnp.broadcast_to(e1w_ref[h, pl.ds(a, 1), :], (BF16_ROWS, tm)).astype(BF16)
                   for h in range(PEER_HEADS)]
            for r in range(0, N_KEYS, BF16_ROWS):
                g = None
                for h in range(PEER_HEADS):
                    t = jnp.where(rk2_ref[h, r:r + BF16_ROWS, :] < c16[h],
                                  e2_ref[h, r:r + BF16_ROWS, :], zero) * w16[h]
                    g = t if g is None else g + t
                tiles.append(g)
                if not chained:
                    continue
                sixteen = jnp.uint32(16)
                bits = pltpu.bitcast(g, jnp.uint32)
                bits = lax.shift_right_logical(lax.shift_right_logical(bits, sixteen), sixteen)
                if len(tiles) % WEIGHT_HOP == 0:
                    bits = pltpu.roll(bits, 1, 1)
                zero = pltpu.bitcast(bits, BF16)
        return jnp.concatenate(tiles, axis=0)

    @pl.when(e == 0)
    def _():
        acc_ref[...] = jnp.zeros_like(acc_ref)
        g_ref[0] = build(0, range(half_a, n_a), chained=False)

    cur = e % 2
    pre = _dot_nt(u_ref[...], hn_ref[...])
    g_own = build(e, range(0, half_a))
    act = (pre * 0.5 * (1.0 + lax.erf(pre * (2.0 ** -0.5)))).astype(BF16)
    p = jnp.concatenate([g_own, g_ref[cur]], axis=0) * act
    acc_ref[...] += jnp.dot(vt_ref[...], p, preferred_element_type=F32)
    g_ref[1 - cur] = build(jnp.minimum(e + 1, n_e - 1), range(half_a, n_a))

    @pl.when(e == n_e - 1)
    def _():
        y_ref[...] = h_ref[...] + acc_ref[...].T


def _experts(hn, u_b, vt_b, e1w, cnt, e2, rk2, h, tm, te):
    m, d = hn.shape
    ne = u_b.shape[0]
    rblk = pl.BlockSpec((PEER_HEADS, N_KEYS, tm), lambda i, e: (0, 0, i))
    return pl.pallas_call(
        functools.partial(_experts_body, n_a=te // N_KEYS),
        grid=(m // tm, ne // te),
        in_specs=[pl.BlockSpec((tm, d), lambda i, e: (i, 0)),
                  pl.BlockSpec((te, d), lambda i, e: (e, 0)),
                  pl.BlockSpec((d, te), lambda i, e: (0, e)),
                  rblk, rblk, rblk, rblk,
                  pl.BlockSpec((tm, d), lambda i, e: (i, 0))],
        out_specs=pl.BlockSpec((tm, d), lambda i, e: (i, 0)),
        out_shape=jax.ShapeDtypeStruct((m, d), F32),
        scratch_shapes=[pltpu.VMEM((d, tm), F32), pltpu.VMEM((2, te // 2, tm), BF16)],
        compiler_params=_cparams(("arbitrary", "arbitrary")),
        name="peer_experts",
    )(hn, u_b, vt_b, e1w, cnt, e2, rk2, h)


PAGES_PER_BLOCK = MOBA_BLOCK // PAGE
SAMPLE_GROUP = 8


def _dot_nn3(a, b):
    ah, al = _split_bf16(a)
    bh, bl = _split_bf16(b)
    dot = functools.partial(jnp.dot, preferred_element_type=F32)
    return dot(ah, bh) + (dot(ah, bl) + dot(al, bh))


def _sample_body(pt_ref, q_ref, kn_ref, vn_ref, *refs, nbp):
    n_pg = SAMPLE_GROUP * PAGES_PER_BLOCK
    k_pages = refs[:n_pg]
    v_pages = refs[n_pg:2 * n_pg]
    o_ref, km_s, m_s, acc_s = refs[2 * n_pg:]
    j = pl.program_id(1)
    dt = q_ref.shape[2]
    grp = SAMPLE_GROUP
    keys = grp * MOBA_BLOCK
    scale = HEAD_DIM ** -0.5
    lane_d = lax.broadcasted_iota(jnp.int32, (HEAD_DIM, LANES), 1)
    lane_t = lax.broadcasted_iota(jnp.int32, (dt, LANES), 1)
    own_blk = (lax.broadcasted_iota(jnp.int32, (grp, keys), 1) // MOBA_BLOCK
               == lax.broadcasted_iota(jnp.int32, (grp, keys), 0))
    ones = jnp.ones((HEAD_DIM, keys), BF16)

    @pl.when(j == 0)
    def _():
        km_s[...] = jnp.zeros_like(km_s)
        m_s[...] = jnp.zeros_like(m_s)

    for h in range(HEADS):
        kt = jnp.concatenate([r[h] for r in k_pages], axis=1)
        vt = jnp.concatenate([r[h] for r in v_pages], axis=1)
        qs = (q_ref[0, h] * scale).astype(BF16)
        s = jnp.dot(qs, kt.astype(BF16), preferred_element_type=F32)
        m = jnp.max(s, axis=-1, keepdims=True)
        p = jnp.exp(s - m)
        p_rows = jnp.concatenate(
            [jnp.where(own_blk, jnp.broadcast_to(p[t:t + 1], (grp, keys)), 0.0) for t in range(dt)],
            axis=0).astype(BF16)
        acc = _dot_nt(p_rows, jnp.concatenate([vt.astype(BF16), ones], axis=0))
        g0 = pl.multiple_of(j * grp, grp)
        for t in range(dt):
            acc_s[h, t, pl.ds(g0, grp), :] = acc[t * grp:(t + 1) * grp]
        m_s[h] = jnp.where(lane_t // grp == j, m, m_s[h])
        km = km_s[h]
        for g in range(grp):
            kmean = jnp.sum(kt[:, g * MOBA_BLOCK:(g + 1) * MOBA_BLOCK], axis=1, keepdims=True)
            km = jnp.where(lane_d == j * grp + g, kmean * (1.0 / MOBA_BLOCK), km)
        km_s[h] = km

    @pl.when(j == pl.num_programs(1) - 1)
    def _():
        tri = (lax.broadcasted_iota(jnp.int32, (dt, dt), 1) <= lax.broadcasted_iota(jnp.int32, (dt, dt), 0))
        own_tok = lax.broadcasted_iota(jnp.int32, (dt, dt * nbp), 1) // nbp == lax.broadcasted_iota(
            jnp.int32, (dt, dt * nbp), 0)
        for h in range(HEADS):
            q = q_ref[0, h]
            gate = jnp.where(lane_t < nbp, _dot_nn3(q, km_s[h]), NEG)
            picked = jnp.zeros(gate.shape, jnp.bool_)
            for _ in range(MOBA_TOPK):
                top = jnp.max(gate, axis=-1, keepdims=True)
                first = jnp.min(jnp.where(gate == top, lane_t, LANES), axis=-1, keepdims=True)
                hit = lane_t == first
                picked = picked | hit
                gate = jnp.where(hit, NEG, gate)
            qs = (q * scale).astype(BF16)
            sn = jnp.where(tri, _dot_nt(qs, kn_ref[0, h].astype(BF16)), NEG)
            mn = jnp.max(sn, axis=-1, keepdims=True)
            pn = jnp.exp(sn - mn)
            ln = jnp.sum(pn, axis=-1, keepdims=True)
            an = jnp.dot(pn.astype(BF16), vn_ref[0, h].astype(BF16), preferred_element_type=F32)
            m_all = m_s[h]
            m_tot = jnp.maximum(jnp.max(jnp.where(picked, m_all, NEG), axis=-1, keepdims=True), mn)
            w = jnp.where(picked, jnp.exp(m_all - m_tot), 0.0)
            wn = jnp.exp(mn - m_tot)
            w_rows = jnp.where(own_tok, jnp.concatenate([w[:, :nbp]] * dt, axis=1), 0.0)
            past = _dot_nn3(w_rows, acc_s[h].reshape(dt * nbp, 2 * HEAD_DIM))
            den = past[:, HEAD_DIM:HEAD_DIM + 1] + wn * ln
            o_ref[0, h] = (past[:, :HEAD_DIM] + wn * an) / den


def _moba_sample(q4, kn4, vn4, cache_k, cache_v, page_table, layer):
    db, _, dt, _ = q4.shape
    nbp = (page_table.shape[1] * PAGE) // MOBA_BLOCK
    assert MOBA_TOPK <= nbp <= LANES and nbp % SAMPLE_GROUP == 0
    kt5 = jnp.transpose(cache_k, (0, 1, 3, 4, 2))
    vt5 = jnp.transpose(cache_v, (0, 1, 3, 4, 2))
    new_blk = pl.BlockSpec((1, HEADS, dt, HEAD_DIM), lambda b, j, pt: (b, 0, 0, 0))
    n_pg = SAMPLE_GROUP * PAGES_PER_BLOCK

    def page_spec(r):
        return pl.BlockSpec((None, None, HEADS, HEAD_DIM, PAGE),
                            lambda b, j, pt: (layer, pt[b, n_pg * j + r], 0, 0, 0))

    pages = [page_spec(r) for r in range(n_pg)]
    return pl.pallas_call(
        functools.partial(_sample_body, nbp=nbp),
        grid_spec=pltpu.PrefetchScalarGridSpec(
            num_scalar_prefetch=1,
            grid=(db, nbp // SAMPLE_GROUP),
            in_specs=[new_blk, new_blk, new_blk] + pages + pages,
            out_specs=new_blk,
            scratch_shapes=[pltpu.VMEM((HEADS, HEAD_DIM, LANES), F32), pltpu.VMEM((HEADS, dt, LANES), F32),
                            pltpu.VMEM((HEADS, dt, nbp, 2 * HEAD_DIM), F32)]),
        out_shape=jax.ShapeDtypeStruct((db, HEADS, dt, HEAD_DIM), F32),
        compiler_params=_cparams(("arbitrary", "arbitrary")),
        name="moba_sample",
    )(page_table, q4, kn4, vn4, *([kt5] * n_pg), *([vt5] * n_pg))


def _mixer_tail(x2d, attn2d, conv2d, w_out_b, g2, wq_t, keys1, keys2, u_b, vt_b, tm, tr, tx, te):
    h, hn = _back(x2d, attn2d, conv2d, w_out_b, g2, tm)
    routing = _route(hn, wq_t, keys1, keys2, tr)
    return _experts(hn, u_b, vt_b, *routing, h, tx, te)


TILE_M = 512
TILE_ROUTE = 256
TILE_X = 512
TILE_E = 1024
CONV_CHUNK = 64


def kernel(x_prompt, x_sample, cache_k, cache_v, state_conv, page_table, norm1_g, w_in, q_norm_g, k_norm_g, conv_w, conv_b, conv_norm_g, w_out, norm2_g, peer_w_query, peer_keys1, peer_keys2, peer_u, peer_v):
    depth = w_in.shape[0]
    bsz, seq, d = x_prompt.shape
    db, dt, _ = x_sample.shape
    assert seq % TILE_M == 0 and TILE_M % MOBA_BLOCK == 0 and d - ATTN_W == ATTN_W
    assert (page_table.shape[1] * PAGE) % MOBA_BLOCK == 0, "partially filled own block is not supported"
    bd = _head_blockdiag()
    xp, xs = x_prompt, x_sample
    rows = {n: [] for n in ("kp", "vp", "cp", "ks", "vs", "cs")}
    for l in range(depth):
        w_in_b = w_in[l].astype(BF16)
        w_out_b = w_out[l].astype(BF16)
        wq_t = peer_w_query[l].T.astype(BF16)
        u_b = peer_u[l].astype(BF16)
        vt_b = peer_v[l].T.astype(BF16)
        g1 = norm1_g[l][None]
        g2 = norm2_g[l][None]
        qg = q_norm_g[l].reshape(1, ATTN_W)
        kg = k_norm_g[l].reshape(1, ATTN_W)
        cb = conv_b[l][None]
        cg = conv_norm_g[l][None]
        tail = functools.partial(_mixer_tail, w_out_b=w_out_b, g2=g2, wq_t=wq_t, keys1=peer_keys1[l],
                                 keys2=peer_keys2[l], u_b=u_b, vt_b=vt_b, te=TILE_E)

        x2d = xp.reshape(bsz * seq, d)
        q, k, v, u, kb, vb, km = _front(x2d, g1, w_in_b, qg, kg, bd, TILE_M, TILE_M // MOBA_BLOCK)
        r3 = lambda t: t.reshape(bsz, seq, ATTN_W)
        attn = _moba_prompt(r3(q), r3(kb), r3(vb), km.reshape(bsz, seq // MOBA_BLOCK, ATTN_W))
        conv = _conv(r3(u), jnp.zeros((bsz, HIST, ATTN_W), F32), conv_w[l], cb, cg, TILE_M, CONV_CHUNK)
        y = tail(x2d, attn.reshape(-1, ATTN_W), conv.reshape(-1, ATTN_W),
                 tm=TILE_M, tr=TILE_ROUTE, tx=TILE_X)
        xp = y.reshape(bsz, seq, d)
        rows["kp"].append(k.reshape(bsz, seq, HEADS, HEAD_DIM))
        rows["vp"].append(v.reshape(bsz, seq, HEADS, HEAD_DIM))
        rows["cp"].append(r3(u)[:, -(CONV_K - 1):])

        n_s = db * dt
        x2d = xs.reshape(n_s, d)
        q, k, v, u, _, _, _ = _front(x2d, g1, w_in_b, qg, kg, bd, n_s, 1)
        h4 = lambda t: t.reshape(db, dt, HEADS, HEAD_DIM)
        hm = lambda t: h4(t).transpose(0, 2, 1, 3)
        attn = _moba_sample(hm(q), hm(k), hm(v), cache_k, cache_v, page_table, l).transpose(0, 2, 1, 3)
        u3 = u.reshape(db, dt, ATTN_W)
        hist = jnp.concatenate([jnp.zeros((db, HIST - (CONV_K - 1), ATTN_W), F32), state_conv[l]], axis=1)
        conv = _conv(u3, hist, conv_w[l], cb, cg, dt, dt)
        y = tail(x2d, attn.reshape(n_s, ATTN_W), conv.reshape(n_s, ATTN_W), tm=n_s, tr=n_s, tx=n_s)
        xs = y.reshape(db, dt, d)
        rows["ks"].append(h4(k))
        rows["vs"].append(h4(v))
        rows["cs"].append(jnp.concatenate([state_conv[l], u3], axis=1)[:, -(CONV_K - 1):])
    return (xp, xs, jnp.stack(rows["kp"]), jnp.stack(rows["vp"]), jnp.stack(rows["cp"]),
            jnp.stack(rows["ks"]), jnp.stack(rows["vs"]), jnp.stack(rows["cs"]))
```

```python
import functools

import jax
import jax.numpy as jnp
from jax import lax
from jax.experimental import pallas as pl
from jax.experimental.pallas import tpu as pltpu

F32 = jnp.float32
BF16 = jnp.bfloat16

HEADS = 8
HEAD_DIM = 64
ATTN_W = HEADS * HEAD_DIM
CONV_K = 31
MOBA_BLOCK = 256
MOBA_TOPK = 3
PAGE = 128
PEER_HEADS = 8
N_KEYS = 128
PEER_TOPK = 16
HALF_KEY = 128
EPS = 1e-6
NEG = float("-inf")

LANES = 128
BF16_ROWS = 16
ROUTE_UNROLL = 2
WEIGHT_HOP = 4
VMEM_LIMIT = 56 * 1024 * 1024


def _cparams(sem):
    return pltpu.CompilerParams(dimension_semantics=sem, vmem_limit_bytes=VMEM_LIMIT)


def _split_bf16(x):
    hi = x.astype(BF16)
    lo = (x - hi.astype(F32)).astype(BF16)
    return hi, lo


def _dot_nt(a, b):
    return lax.dot_general(a, b, (((1,), (1,)), ((), ())), preferred_element_type=F32)


def _dot_nt3(a, b):
    ah, al = _split_bf16(a)
    bh, bl = _split_bf16(b)
    return _dot_nt(ah, bh) + (_dot_nt(ah, bl) + _dot_nt(al, bh))


def _head_blockdiag():
    r = lax.broadcasted_iota(jnp.int32, (ATTN_W, ATTN_W), 0) // HEAD_DIM
    c = lax.broadcasted_iota(jnp.int32, (ATTN_W, ATTN_W), 1) // HEAD_DIM
    return jnp.where(r == c, 1.0 / HEAD_DIM, 0.0).astype(BF16)


def _front_body(x_ref, g1_ref, w_ref, qg_ref, kg_ref, bd_ref,
                q_ref, k_ref, v_ref, u_ref, kb_ref, vb_ref, km_ref, *, n_sub, kv_transposed):
    x = x_ref[...]
    ms = jnp.mean(x * x, axis=-1, keepdims=True)
    xn = (x * lax.rsqrt(ms + EPS)) * g1_ref[...]
    proj = jnp.dot(xn.astype(BF16), w_ref[...], preferred_element_type=F32)
    bd = bd_ref[...]

    def head_norm(t, g):
        hi, lo = _split_bf16(t * t)
        msq = jnp.dot(hi, bd, preferred_element_type=F32) + jnp.dot(lo, bd, preferred_element_type=F32)
        return (t * lax.rsqrt(msq + EPS)) * g

    q_ref[...] = head_norm(proj[:, :ATTN_W], qg_ref[...])
    k = head_norm(proj[:, ATTN_W:2 * ATTN_W], kg_ref[...])
    v = proj[:, 2 * ATTN_W:3 * ATTN_W]
    if kv_transposed:
        k_ref[0] = k.T
        v_ref[0] = v.T
    else:
        k_ref[...] = k
        v_ref[...] = v
    kb_ref[...] = k.astype(BF16)
    vb_ref[...] = v.astype(BF16)
    a = proj[:, 3 * ATTN_W:3 * ATTN_W + ATTN_W]
    gt = proj[:, 4 * ATTN_W:]
    u_ref[...] = a * jax.nn.sigmoid(gt)
    rows = k.shape[0] // n_sub
    for s in range(n_sub):
        km_ref[0, s:s + 1, :] = jnp.sum(k[s * rows:(s + 1) * rows], axis=0, keepdims=True) * (1.0 / rows)


def _front(x2d, g1, w_in_b, qg, kg, bd, tm, n_sub, seq=None):
    m, d = x2d.shape
    nw = w_in_b.shape[1]
    row = lambda i: (i, 0)
    fixed = lambda i: (0, 0)
    if seq is None:
        kv_shape = jax.ShapeDtypeStruct((m, ATTN_W), F32)
        kv_blk = pl.BlockSpec((tm, ATTN_W), row)
    else:
        tiles = seq // tm
        kv_shape = jax.ShapeDtypeStruct((m // seq, ATTN_W, seq), F32)
        kv_blk = pl.BlockSpec((1, ATTN_W, tm), lambda i: (i // tiles, 0, i % tiles))
    outs = (
        jax.ShapeDtypeStruct((m, ATTN_W), F32),
        kv_shape,
        kv_shape,
        jax.ShapeDtypeStruct((m, ATTN_W), F32),
        jax.ShapeDtypeStruct((m, ATTN_W), BF16),
        jax.ShapeDtypeStruct((m, ATTN_W), BF16),
        jax.ShapeDtypeStruct((m // tm, n_sub, ATTN_W), F32),
    )
    blk = pl.BlockSpec((tm, ATTN_W), row)
    return pl.pallas_call(
        functools.partial(_front_body, n_sub=n_sub, kv_transposed=seq is not None),
        grid=(m // tm,),
        in_specs=[pl.BlockSpec((tm, d), row), pl.BlockSpec((1, d), fixed),
                  pl.BlockSpec((d, nw), fixed), pl.BlockSpec((1, ATTN_W), fixed),
                  pl.BlockSpec((1, ATTN_W), fixed), pl.BlockSpec((ATTN_W, ATTN_W), fixed)],
        out_specs=(blk, kv_blk, kv_blk, blk, blk, blk,
                   pl.BlockSpec((1, n_sub, ATTN_W), lambda i: (i, 0, 0))),
        out_shape=outs,
        compiler_params=_cparams(("arbitrary",)),
        name="front",
    )(x2d, g1, w_in_b, qg, kg, bd)


HIST = 32


SUBLANES = 8


def _conv_body(u_ref, h0_ref, w_ref, b_ref, g_ref, o_ref, buf_ref, sh_ref, *, ts, chunk):
    t = pl.program_id(1)

    @pl.when(t == 0)
    def _():
        buf_ref[0:HIST, :] = h0_ref[0]

    @pl.when(t != 0)
    def _():
        buf_ref[0:HIST, :] = buf_ref[ts:ts + HIST, :]

    buf_ref[HIST:HIST + ts, :] = u_ref[0]
    span = sh_ref.shape[1]
    for r in range(1, SUBLANES):
        sh_ref[r - 1] = buf_ref[r:r + span, :]
    w = w_ref[...]
    off = HIST - (CONV_K - 1)
    for c in range(ts // chunk):
        acc = jnp.zeros((chunk, u_ref.shape[2]), F32)
        for j in range(CONV_K):
            a, r = divmod(off + j, SUBLANES)
            lo = c * chunk + a * SUBLANES
            rows = buf_ref[lo:lo + chunk, :] if r == 0 else sh_ref[r - 1, lo:lo + chunk, :]
            acc = acc + rows * w[j:j + 1, :]
        y = acc + b_ref[...]
        ms = jnp.mean(y * y, axis=-1, keepdims=True)
        y = (y * lax.rsqrt(ms + EPS)) * g_ref[...]
        o_ref[0, c * chunk:(c + 1) * chunk, :] = y * jax.nn.sigmoid(y)


def _conv(u3, hist0, conv_w, conv_b, conv_g, ts, chunk):
    b, l, c = u3.shape
    fixed = lambda i, t: (0, 0)
    return pl.pallas_call(
        functools.partial(_conv_body, ts=ts, chunk=chunk),
        grid=(b, l // ts),
        in_specs=[pl.BlockSpec((1, ts, c), lambda i, t: (i, t, 0)),
                  pl.BlockSpec((1, HIST, c), lambda i, t: (i, 0, 0)),
                  pl.BlockSpec((CONV_K, c), fixed), pl.BlockSpec((1, c), fixed),
                  pl.BlockSpec((1, c), fixed)],
        out_specs=pl.BlockSpec((1, ts, c), lambda i, t: (i, t, 0)),
        out_shape=jax.ShapeDtypeStruct((b, l, c), F32),
        scratch_shapes=[pltpu.VMEM((ts + HIST, c), F32),
                        pltpu.VMEM((SUBLANES - 1, ts + HIST - SUBLANES, c), F32)],
        compiler_params=_cparams(("arbitrary", "arbitrary")),
        name="conv_branch",
    )(u3, hist0, conv_w, conv_b, conv_g)


def _block_select(gate, blk_idx, n_valid, nb):
    height = gate.shape[0]
    valid = blk_idx < n_valid
    g = jnp.where(valid, gate, NEG)
    rank = jnp.zeros(gate.shape, jnp.int32)
    for s in range(1, nb):
        fwd = pltpu.roll(g, height - s, 0)
        bwd = pltpu.roll(g, nb - s, 0)
        wraps = blk_idx + s >= nb
        other = jnp.where(wraps, bwd, fwd)
        beats = (other > g) | ((other == g) & wraps)
        rank = rank + beats.astype(jnp.int32)
    return valid & (rank < MOBA_TOPK)


MASK_BIAS = -1e30
ATTN_GROUP = 4


def _attn_body(q_ref, kb_ref, vb_ref, km_ref, ind_ref, o_ref, s_ref, mx_ref, acc_ref, *, nb):
    i = pl.program_id(1)
    blk = MOBA_BLOCK
    hps = LANES // HEAD_DIM
    q = q_ref[0]
    km = km_ref[0]
    pad_rows = LANES - HEADS * nb
    kmt = jnp.concatenate([km] * HEADS + ([jnp.zeros((pad_rows, ATTN_W), F32)] if pad_rows else []), axis=0)
    r_head = lax.broadcasted_iota(jnp.int32, kmt.shape, 0) // nb
    c_head = lax.broadcasted_iota(jnp.int32, kmt.shape, 1) // HEAD_DIM
    kmt = jnp.where(r_head == c_head, kmt, 0.0)
    gate_t = _dot_nt3(kmt, q)
    rix = lax.broadcasted_iota(jnp.int32, (LANES, blk), 0)
    n_valid = jnp.where(rix < HEADS * nb, i, 0)
    sel_t = _block_select(gate_t, rix % nb, n_valid, nb)
    bias = jnp.where(sel_t, 0.0, MASK_BIAS).T
    lane = lax.broadcasted_iota(jnp.int32, (blk, LANES), 1)

    row = lax.broadcasted_iota(jnp.int32, (hps * blk, blk), 0) % blk
    col = lax.broadcasted_iota(jnp.int32, (hps * blk, blk), 1)
    causal = col <= row
    scale = HEAD_DIM ** -0.5
    own0 = pl.multiple_of(i * blk, blk)
    grp = ATTN_GROUP * blk
    n_grp = (i + ATTN_GROUP) // ATTN_GROUP

    def tilemax(s):
        out = s[:, :LANES]
        for t in range(1, s.shape[1] // LANES):
            out = jnp.maximum(out, s[:, t * LANES:(t + 1) * LANES])
        return out

    for c in range(ATTN_W // LANES):
        slab = slice(c * LANES, (c + 1) * LANES)
        qs = q[:, slab] * scale
        stacked = []
        for half in range(hps):
            h = c * hps + half
            qm = jnp.where((lane // HEAD_DIM) == half, qs, 0.0).astype(BF16)
            bh = jnp.where((lane // nb) == h, bias, 0.0).astype(BF16)
            stacked.append(jnp.concatenate([qm, bh], axis=1))
        q2 = jnp.concatenate(stacked, axis=0)

        mx_ref[...] = jnp.full(mx_ref.shape, NEG, F32)

        def scores(g, carry, q2=q2, slab=slab):
            g0 = pl.multiple_of(g * grp, grp)
            kj = jnp.concatenate([kb_ref[0, pl.ds(g0, grp), slab], ind_ref[pl.ds(g0, grp), :]], axis=1)
            s = _dot_nt(q2, kj)
            s_ref[:, pl.ds(g0, grp)] = s
            mx_ref[...] = jnp.maximum(mx_ref[...], tilemax(s))
            return carry

        lax.fori_loop(0, n_grp, scores, 0)
        s = jnp.where(causal, _dot_nt(q2[:, :LANES], kb_ref[0, pl.ds(own0, blk), slab]), NEG)
        s_ref[:, pl.ds(own0, blk)] = s
        m = jnp.max(jnp.maximum(mx_ref[...], tilemax(s)), axis=-1, keepdims=True)
        acc_ref[...] = jnp.zeros_like(acc_ref)

        def weigh(g, carry, m=m, slab=slab):
            g0 = pl.multiple_of(g * grp, grp)
            p = jnp.exp((s_ref[:, pl.ds(g0, grp)] - m).astype(BF16))
            vv = jnp.concatenate([vb_ref[0, pl.ds(g0, grp), slab], jnp.ones((grp, LANES), BF16)], axis=1)
            acc_ref[...] += jnp.dot(p, vv, preferred_element_type=F32)
            return carry

        lax.fori_loop(0, n_grp, weigh, 0)
        o = acc_ref[:, :LANES] / acc_ref[:, LANES:]
        out = o[0:blk]
        for half in range(1, hps):
            out = jnp.where((lane // HEAD_DIM) == half, o[half * blk:(half + 1) * blk], out)
        o_ref[0, :, slab] = out


def _moba_prompt(q3, kb3, vb3, km3):
    b, s, w = q3.shape
    nb = s // MOBA_BLOCK
    assert HEADS * nb <= LANES and nb % ATTN_GROUP == 0
    rows = (LANES // HEAD_DIM) * MOBA_BLOCK
    key_blk = lax.broadcasted_iota(jnp.int32, (s, LANES), 0) // MOBA_BLOCK
    lane = lax.broadcasted_iota(jnp.int32, (s, LANES), 1)
    ind = ((key_blk == lane % nb) & (lane < HEADS * nb)).astype(BF16)
    return pl.pallas_call(
        functools.partial(_attn_body, nb=nb),
        grid=(b, nb),
        in_specs=[pl.BlockSpec((1, MOBA_BLOCK, w), lambda bi, i: (bi, i, 0)),
                  pl.BlockSpec((1, s, w), lambda bi, i: (bi, 0, 0)),
                  pl.BlockSpec((1, s, w), lambda bi, i: (bi, 0, 0)),
                  pl.BlockSpec((1, nb, w), lambda bi, i: (bi, 0, 0)),
                  pl.BlockSpec((s, LANES), lambda bi, i: (0, 0))],
        out_specs=pl.BlockSpec((1, MOBA_BLOCK, w), lambda bi, i: (bi, i, 0)),
        out_shape=jax.ShapeDtypeStruct((b, s, w), F32),
        scratch_shapes=[pltpu.VMEM((rows, s), F32), pltpu.VMEM((rows, LANES), F32),
                        pltpu.VMEM((rows, 2 * LANES), F32)],
        compiler_params=_cparams(("arbitrary", "arbitrary")),
        name="moba_prompt",
    )(q3, kb3, vb3, km3, ind)


def _back_body(x_ref, a_ref, c_ref, w_ref, g2_ref, h_ref, hn_ref):
    mix = jnp.concatenate([a_ref[...], c_ref[...]], axis=-1).astype(BF16)
    h = x_ref[...] + jnp.dot(mix, w_ref[...], preferred_element_type=F32)
    h_ref[...] = h
    ms = jnp.mean(h * h, axis=-1, keepdims=True)
    hn_ref[...] = ((h * lax.rsqrt(ms + EPS)) * g2_ref[...]).astype(BF16)


def _back(x2d, attn2d, conv2d, w_out_b, g2, tm):
    m, d = x2d.shape
    row = lambda i: (i, 0)
    fixed = lambda i: (0, 0)
    return pl.pallas_call(
        _back_body,
        grid=(m // tm,),
        in_specs=[pl.BlockSpec((tm, d), row), pl.BlockSpec((tm, ATTN_W), row),
                  pl.BlockSpec((tm, d - ATTN_W), row), pl.BlockSpec((d, d), fixed),
                  pl.BlockSpec((1, d), fixed)],
        out_specs=(pl.BlockSpec((tm, d), row), pl.BlockSpec((tm, d), row)),
        out_shape=(jax.ShapeDtypeStruct((m, d), F32), jax.ShapeDtypeStruct((m, d), BF16)),
        compiler_params=_cparams(("arbitrary",)),
        name="back",
    )(x2d, attn2d, conv2d, w_out_b, g2)


def _pair_candidates():
    return [(r1, r2) for r1 in range(PEER_TOPK) for r2 in range(PEER_TOPK)
            if (r1 + 1) * (r2 + 1) <= PEER_TOPK]


RANK_MARK = 2.0 ** 100


def _top_rounds(s, k, exact):
    n = s.shape[0]
    idx = lax.broadcasted_iota(jnp.int32, s.shape, 0)
    vals = []
    for r in range(k):
        m = jnp.max(s, axis=0, keepdims=True)
        hit = s == m
        if exact:
            first = jnp.min(jnp.where(hit, idx, n), axis=0, keepdims=True)
            hit = idx == first
        s = jnp.where(hit, -(r + 1.0) * RANK_MARK, s)
        vals.append(m)
    rank = jnp.where(s <= -RANK_MARK, s * (-1.0 / RANK_MARK) - 1.0, float(k))
    count = jnp.sum((rank < k).astype(F32), axis=0, keepdims=True)
    return vals, rank, count


def _route_body(hn_ref, wq_ref, k1_ref, k2_ref, e1w_ref, cnt_ref, e2_ref, rk2_ref, qh_ref):
    tm = hn_ref.shape[0]
    qh_ref[...] = _dot_nt(wq_ref[...], hn_ref[...])
    pairs = _pair_candidates()
    n_pad = (-len(pairs)) % 8

    def route_head(h, exact):
        base = pl.multiple_of(h * (2 * HALF_KEY), 2 * HALF_KEY)
        q1h, q1l = _split_bf16(qh_ref[pl.ds(base, HALF_KEY), :])
        q2h, q2l = _split_bf16(qh_ref[pl.ds(base + HALF_KEY, HALF_KEY), :])
        k1h, k1l = _split_bf16(k1_ref[h])
        k2h, k2l = _split_bf16(k2_ref[h])
        dot = functools.partial(jnp.dot, preferred_element_type=F32)
        s1 = dot(k1h, q1h) + (dot(k1h, q1l) + dot(k1l, q1h))
        s2 = dot(k2h, q2h) + (dot(k2h, q2l) + dot(k2l, q2h))
        v1, rk1, n1 = _top_rounds(s1, PEER_TOPK, exact)
        v2, rk2, n2 = _top_rounds(s2, PEER_TOPK, exact)
        cand = jnp.concatenate([v1[r1] + v2[r2] for r1, r2 in pairs]
                               + [jnp.full((n_pad, tm), NEG, F32)], axis=0)
        _, rkc, nc = _top_rounds(cand, PEER_TOPK, exact)
        picked = (rkc < PEER_TOPK).astype(F32)
        ez = jnp.exp(cand - cand[0:1, :]) * picked
        z = jnp.sum(ez, axis=0, keepdims=True)
        cnt = jnp.zeros((N_KEYS, tm), F32)
        for r1 in range(PEER_TOPK):
            rows = [i for i, (a, _) in enumerate(pairs) if a == r1]
            c_r1 = jnp.sum(picked[rows[0]:rows[-1] + 1, :], axis=0, keepdims=True)
            cnt = jnp.where(rk1 == r1, c_r1, cnt)
        e1w_ref[h] = jnp.exp(s1 - v1[0]) / z
        cnt_ref[h] = cnt
        e2_ref[h] = jnp.exp(s2 - v2[0]).astype(BF16)
        rk2_ref[h] = rk2.astype(F32).astype(BF16)
        return jnp.max(jnp.abs(n1 - PEER_TOPK) + jnp.abs(n2 - PEER_TOPK) + jnp.abs(nc - PEER_TOPK))

    def heads(i, carry):
        hs = [i * ROUTE_UNROLL + k for k in range(ROUTE_UNROLL)]
        n_tied = [route_head(h, exact=False) for h in hs]
        for h, n in zip(hs, n_tied):
            @pl.when(n > 0.0)
            def _(h=h):
                route_head(h, exact=True)

        return carry

    lax.fori_loop(0, PEER_HEADS // ROUTE_UNROLL, heads, 0)


def _route(hn, wq_t, keys1, keys2, tm):
    m, d = hn.shape
    nq = wq_t.shape[0]
    out = jax.ShapeDtypeStruct((PEER_HEADS, N_KEYS, m), F32)
    outb = jax.ShapeDtypeStruct((PEER_HEADS, N_KEYS, m), BF16)
    oblk = pl.BlockSpec((PEER_HEADS, N_KEYS, tm), lambda i: (0, 0, i))
    kblk = pl.BlockSpec((PEER_HEADS, N_KEYS, HALF_KEY), lambda i: (0, 0, 0))
    return pl.pallas_call(
        _route_body,
        grid=(m // tm,),
        in_specs=[pl.BlockSpec((tm, d), lambda i: (i, 0)), pl.BlockSpec((nq, d), lambda i: (0, 0)),
                  kblk, kblk],
        out_specs=(oblk, oblk, oblk, oblk),
        out_shape=(out, out, outb, outb),
        scratch_shapes=[pltpu.VMEM((nq, tm), F32)],
        compiler_params=_cparams(("arbitrary",)),
        name="peer_route",
    )(hn, wq_t, keys1, keys2)


def _experts_body(hn_ref, u_ref, vt_ref, e1w_ref, cnt_ref, e2_ref, rk2_ref, h_ref, y_ref, acc_ref, g_ref, *, n_a):
    e = pl.program_id(1)
    n_e = pl.num_programs(1)
    tm = hn_ref.shape[0]
    half_a = n_a // 2

    def build(tile, key1s, chained=True):
        zero = jnp.zeros((), BF16)
        tiles = []
        for al in key1s:
            a = tile * n_a + al
            c16 = [jnp.broadcast_to(cnt_ref[h, pl.ds(a, 1), :], (BF16_ROWS, tm)).astype(BF16)
                   for h in range(PEER_HEADS)]
            w16 = [jnp.broadcast_to(e1w_ref[h, pl.ds(a, 1), :], (BF16_ROWS, tm)).astype(BF16)
                   for h in range(PEER_HEADS)]
            for r in range(0, N_KEYS, BF16_ROWS):
                g = None
                for h in range(PEER_HEADS):
                    t = jnp.where(rk2_ref[h, r:r + BF16_ROWS, :] < c16[h],
                                  e2_ref[h, r:r + BF16_ROWS, :], zero) * w16[h]
                    g = t if g is None else g + t
                tiles.append(g)
                if not chained:
                    continue
                sixteen = jnp.uint32(16)
                bits = pltpu.bitcast(g, jnp.uint32)
                bits = lax.shift_right_logical(lax.shift_right_logical(bits, sixteen), sixteen)
                if len(tiles) % WEIGHT_HOP == 0:
                    bits = pltpu.roll(bits, 1, 1)
                zero = pltpu.bitcast(bits, BF16)
        return jnp.concatenate(tiles, axis=0)

    @pl.when(e == 0)
    def _():
        acc_ref[...] = jnp.zeros_like(acc_ref)
        g_ref[0] = build(0, range(half_a, n_a), chained=False)

    cur = e % 2
    pre = _dot_nt(u_ref[...], hn_ref[...])
    g_own = build(e, range(0, half_a))
    act = (pre * 0.5 * (1.0 + lax.erf(pre * (2.0 ** -0.5)))).astype(BF16)
    p = jnp.concatenate([g_own, g_ref[cur]], axis=0) * act
    acc_ref[...] += jnp.dot(vt_ref[...], p, preferred_element_type=F32)
    g_ref[1 - cur] = build(jnp.minimum(e + 1, n_e - 1), range(half_a, n_a))

    @pl.when(e == n_e - 1)
    def _():
        y_ref[...] = h_ref[...] + acc_ref[...].T


def _experts(hn, u_b, vt_b, e1w, cnt, e2, rk2, h, tm, te):
    m, d = hn.shape
    ne = u_b.shape[0]
    rblk = pl.BlockSpec((PEER_HEADS, N_KEYS, tm), lambda i, e: (0, 0, i))
    return pl.pallas_call(
        functools.partial(_experts_body, n_a=te // N_KEYS),
        grid=(m // tm, ne // te),
        in_specs=[pl.BlockSpec((tm, d), lambda i, e: (i, 0)),
                  pl.BlockSpec((te, d), lambda i, e: (e, 0)),
                  pl.BlockSpec((d, te), lambda i, e: (0, e)),
                  rblk, rblk, rblk, rblk,
                  pl.BlockSpec((tm, d), lambda i, e: (i, 0))],
        out_specs=pl.BlockSpec((tm, d), lambda i, e: (i, 0)),
        out_shape=jax.ShapeDtypeStruct((m, d), F32),
        scratch_shapes=[pltpu.VMEM((d, tm), F32), pltpu.VMEM((2, te // 2, tm), BF16)],
        compiler_params=_cparams(("arbitrary", "arbitrary")),
        name="peer_experts",
    )(hn, u_b, vt_b, e1w, cnt, e2, rk2, h)


PAGES_PER_BLOCK = MOBA_BLOCK // PAGE
SAMPLE_GROUP = 8


def _dot_nn3(a, b):
    ah, al = _split_bf16(a)
    bh, bl = _split_bf16(b)
    dot = functools.partial(jnp.dot, preferred_element_type=F32)
    return dot(ah, bh) + (dot(ah, bl) + dot(al, bh))


def _sample_body(pt_ref, q_ref, kn_ref, vn_ref, *refs, nbp):
    n_pg = SAMPLE_GROUP * PAGES_PER_BLOCK
    k_pages = refs[:n_pg]
    v_pages = refs[n_pg:2 * n_pg]
    o_ref, km_s, m_s, acc_s = refs[2 * n_pg:]
    j = pl.program_id(1)
    dt = q_ref.shape[2]
    grp = SAMPLE_GROUP
    keys = grp * MOBA_BLOCK
    scale = HEAD_DIM ** -0.5
    lane_d = lax.broadcasted_iota(jnp.int32, (HEAD_DIM, LANES), 1)
    lane_t = lax.broadcasted_iota(jnp.int32, (dt, LANES), 1)
    own_blk = (lax.broadcasted_iota(jnp.int32, (grp, keys), 1) // MOBA_BLOCK
               == lax.broadcasted_iota(jnp.int32, (grp, keys), 0))
    ones = jnp.ones((HEAD_DIM, keys), BF16)

    @pl.when(j == 0)
    def _():
        km_s[...] = jnp.zeros_like(km_s)
        m_s[...] = jnp.zeros_like(m_s)

    for h in range(HEADS):
        kt = jnp.concatenate([r[h] for r in k_pages], axis=1)
        vt = jnp.concatenate([r[h] for r in v_pages], axis=1)
        qs = (q_ref[0, h] * scale).astype(BF16)
        s = jnp.dot(qs, kt.astype(BF16), preferred_element_type=F32)
        m = jnp.max(s, axis=-1, keepdims=True)
        p = jnp.exp(s - m)
        p_rows = jnp.concatenate(
            [jnp.where(own_blk, jnp.broadcast_to(p[t:t + 1], (grp, keys)), 0.0) for t in range(dt)],
            axis=0).astype(BF16)
        acc = _dot_nt(p_rows, jnp.concatenate([vt.astype(BF16), ones], axis=0))
        g0 = pl.multiple_of(j * grp, grp)
        for t in range(dt):
            acc_s[h, t, pl.ds(g0, grp), :] = acc[t * grp:(t + 1) * grp]
        m_s[h] = jnp.where(lane_t // grp == j, m, m_s[h])
        km = km_s[h]
        for g in range(grp):
            kmean = jnp.sum(kt[:, g * MOBA_BLOCK:(g + 1) * MOBA_BLOCK], axis=1, keepdims=True)
            km = jnp.where(lane_d == j * grp + g, kmean * (1.0 / MOBA_BLOCK), km)
        km_s[h] = km

    @pl.when(j == pl.num_programs(1) - 1)
    def _():
        tri = (lax.broadcasted_iota(jnp.int32, (dt, dt), 1) <= lax.broadcasted_iota(jnp.int32, (dt, dt), 0))
        own_tok = lax.broadcasted_iota(jnp.int32, (dt, dt * nbp), 1) // nbp == lax.broadcasted_iota(
            jnp.int32, (dt, dt * nbp), 0)
        for h in range(HEADS):
            q = q_ref[0, h]
            gate = jnp.where(lane_t < nbp, _dot_nn3(q, km_s[h]), NEG)
            picked = jnp.zeros(gate.shape, jnp.bool_)
            for _ in range(MOBA_TOPK):
                top = jnp.max(gate, axis=-1, keepdims=True)
                first = jnp.min(jnp.where(gate == top, lane_t, LANES), axis=-1, keepdims=True)
                hit = lane_t == first
                picked = picked | hit
                gate = jnp.where(hit, NEG, gate)
            qs = (q * scale).astype(BF16)
            sn = jnp.where(tri, _dot_nt(qs, kn_ref[0, h].astype(BF16)), NEG)
            mn = jnp.max(sn, axis=-1, keepdims=True)
            pn = jnp.exp(sn - mn)
            ln = jnp.sum(pn, axis=-1, keepdims=True)
            an = jnp.dot(pn.astype(BF16), vn_ref[0, h].astype(BF16), preferred_element_type=F32)
            m_all = m_s[h]
            m_tot = jnp.maximum(jnp.max(jnp.where(picked, m_all, NEG), axis=-1, keepdims=True), mn)
            w = jnp.where(picked, jnp.exp(m_all - m_tot), 0.0)
            wn = jnp.exp(mn - m_tot)
            w_rows = jnp.where(own_tok, jnp.concatenate([w[:, :nbp]] * dt, axis=1), 0.0)
            past = _dot_nn3(w_rows, acc_s[h].reshape(dt * nbp, 2 * HEAD_DIM))
            den = past[:, HEAD_DIM:HEAD_DIM + 1] + wn * ln
            o_ref[0, h] = (past[:, :HEAD_DIM] + wn * an) / den


def _moba_sample(q4, kn4, vn4, cache_k, cache_v, page_table, layer):
    db, _, dt, _ = q4.shape
    nbp = (page_table.shape[1] * PAGE) // MOBA_BLOCK
    assert MOBA_TOPK <= nbp <= LANES and nbp % SAMPLE_GROUP == 0
    kt5 = jnp.transpose(cache_k, (0, 1, 3, 4, 2))
    vt5 = jnp.transpose(cache_v, (0, 1, 3, 4, 2))
    new_blk = pl.BlockSpec((1, HEADS, dt, HEAD_DIM), lambda b, j, pt: (b, 0, 0, 0))
    n_pg = SAMPLE_GROUP * PAGES_PER_BLOCK

    def page_spec(r):
        return pl.BlockSpec((None, None, HEADS, HEAD_DIM, PAGE),
                            lambda b, j, pt: (layer, pt[b, n_pg * j + r], 0, 0, 0))

    pages = [page_spec(r) for r in range(n_pg)]
    return pl.pallas_call(
        functools.partial(_sample_body, nbp=nbp),
        grid_spec=pltpu.PrefetchScalarGridSpec(
            num_scalar_prefetch=1,
            grid=(db, nbp // SAMPLE_GROUP),
            in_specs=[new_blk, new_blk, new_blk] + pages + pages,
            out_specs=new_blk,
            scratch_shapes=[pltpu.VMEM((HEADS, HEAD_DIM, LANES), F32), pltpu.VMEM((HEADS, dt, LANES), F32),
                            pltpu.VMEM((HEADS, dt, nbp, 2 * HEAD_DIM), F32)]),
        out_shape=jax.ShapeDtypeStruct((db, HEADS, dt, HEAD_DIM), F32),
        compiler_params=_cparams(("arbitrary", "arbitrary")),
        name="moba_sample",
    )(page_table, q4, kn4, vn4, *([kt5] * n_pg), *([vt5] * n_pg))


def _mixer_tail(x2d, attn2d, conv2d, w_out_b, g2, wq_t, keys1, keys2, u_b, vt_b, tm, tr, tx, te):
    h, hn = _back(x2d, attn2d, conv2d, w_out_b, g2, tm)
    routing = _route(hn, wq_t, keys1, keys2, tr)
    return _experts(hn, u_b, vt_b, *routing, h, tx, te)


TILE_M = 512
TILE_ROUTE = 256
TILE_X = 512
TILE_E = 1024
CONV_CHUNK = 64


def kernel(x_prompt, x_sample, cache_k, cache_v, state_conv, page_table, norm1_g, w_in, q_norm_g, k_norm_g, conv_w, conv_b, conv_norm_g, w_out, norm2_g, peer_w_query, peer_keys1, peer_keys2, peer_u, peer_v):
    depth = w_in.shape[0]
    bsz, seq, d = x_prompt.shape
    db, dt, _ = x_sample.shape
    assert seq % TILE_M == 0 and TILE_M % MOBA_BLOCK == 0 and d - ATTN_W == ATTN_W
    assert (page_table.shape[1] * PAGE) % MOBA_BLOCK == 0, "partially filled own block is not supported"
    bd = _head_blockdiag()
    xp, xs = x_prompt, x_sample
    rows = {n: [] for n in ("kp", "vp", "cp", "ks", "vs", "cs")}
    for l in range(depth):
        w_in_b = w_in[l].astype(BF16)
        w_out_b = w_out[l].astype(BF16)
        wq_t = peer_w_query[l].T.astype(BF16)
        u_b = peer_u[l].astype(BF16)
        vt_b = peer_v[l].T.astype(BF16)
        g1 = norm1_g[l][None]
        g2 = norm2_g[l][None]
        qg = q_norm_g[l].reshape(1, ATTN_W)
        kg = k_norm_g[l].reshape(1, ATTN_W)
        cb = conv_b[l][None]
        cg = conv_norm_g[l][None]
        tail = functools.partial(_mixer_tail, w_out_b=w_out_b, g2=g2, wq_t=wq_t, keys1=peer_keys1[l],
                                 keys2=peer_keys2[l], u_b=u_b, vt_b=vt_b, te=TILE_E)

        x2d = xp.reshape(bsz * seq, d)
        q, k_t, v_t, u, kb, vb, km = _front(x2d, g1, w_in_b, qg, kg, bd, TILE_M, TILE_M // MOBA_BLOCK, seq=seq)
        r3 = lambda t: t.reshape(bsz, seq, ATTN_W)
        from_t = lambda t: t.reshape(bsz, HEADS, HEAD_DIM, seq).transpose(0, 3, 1, 2)
        attn = _moba_prompt(r3(q), r3(kb), r3(vb), km.reshape(bsz, seq // MOBA_BLOCK, ATTN_W))
        conv = _conv(r3(u), jnp.zeros((bsz, HIST, ATTN_W), F32), conv_w[l], cb, cg, TILE_M, CONV_CHUNK)
        y = tail(x2d, attn.reshape(-1, ATTN_W), conv.reshape(-1, ATTN_W),
                 tm=TILE_M, tr=TILE_ROUTE, tx=TILE_X)
        xp = y.reshape(bsz, seq, d)
        rows["kp"].append(from_t(k_t))
        rows["vp"].append(from_t(v_t))
        rows["cp"].append(r3(u)[:, -(CONV_K - 1):])

        n_s = db * dt
        x2d = xs.reshape(n_s, d)
        q, k, v, u, _, _, _ = _front(x2d, g1, w_in_b, qg, kg, bd, n_s, 1)
        h4 = lambda t: t.reshape(db, dt, HEADS, HEAD_DIM)
        hm = lambda t: h4(t).transpose(0, 2, 1, 3)
        attn = _moba_sample(hm(q), hm(k), hm(v), cache_k, cache_v, page_table, l).transpose(0, 2, 1, 3)
        u3 = u.reshape(db, dt, ATTN_W)
        hist = jnp.concatenate([jnp.zeros((db, HIST - (CONV_K - 1), ATTN_W), F32), state_conv[l]], axis=1)
        conv = _conv(u3, hist, conv_w[l], cb, cg, dt, dt)
        y = tail(x2d, attn.reshape(n_s, ATTN_W), conv.reshape(n_s, ATTN_W), tm=n_s, tr=n_s, tx=n_s)
        xs = y.reshape(db, dt, d)
        rows["ks"].append(h4(k))
        rows["vs"].append(h4(v))
        rows["cs"].append(jnp.concatenate([state_conv[l], u3], axis=1)[:, -(CONV_K - 1):])
    return (xp, xs, jnp.stack(rows["kp"]), jnp.stack(rows["vp"]), jnp.stack(rows["cp"]),
            jnp.stack(rows["ks"]), jnp.stack(rows["vs"]), jnp.stack(rows["cs"]))
```

```python
import functools

import jax
import jax.numpy as jnp
from jax import lax
from jax.experimental import pallas as pl
from jax.experimental.pallas import tpu as pltpu

F32 = jnp.float32
BF16 = jnp.bfloat16

HEADS = 8
HEAD_DIM = 64
ATTN_W = HEADS * HEAD_DIM
CONV_K = 31
MOBA_BLOCK = 256
MOBA_TOPK = 3
PAGE = 128
PEER_HEADS = 8
N_KEYS = 128
PEER_TOPK = 16
HALF_KEY = 128
EPS = 1e-6
NEG = float("-inf")

LANES = 128
BF16_ROWS = 16
ROUTE_UNROLL = 4
WEIGHT_HOP = 4
VMEM_LIMIT = 56 * 1024 * 1024


def _cparams(sem):
    return pltpu.CompilerParams(dimension_semantics=sem, vmem_limit_bytes=VMEM_LIMIT)


def _split_bf16(x):
    hi = x.astype(BF16)
    lo = (x - hi.astype(F32)).astype(BF16)
    return hi, lo


def _dot_nt(a, b):
    return lax.dot_general(a, b, (((1,), (1,)), ((), ())), preferred_element_type=F32)


def _dot_nt3(a, b):
    ah, al = _split_bf16(a)
    bh, bl = _split_bf16(b)
    return _dot_nt(ah, bh) + (_dot_nt(ah, bl) + _dot_nt(al, bh))


def _head_blockdiag():
    r = lax.broadcasted_iota(jnp.int32, (ATTN_W, ATTN_W), 0) // HEAD_DIM
    c = lax.broadcasted_iota(jnp.int32, (ATTN_W, ATTN_W), 1) // HEAD_DIM
    return jnp.where(r == c, 1.0 / HEAD_DIM, 0.0).astype(BF16)


def _front_body(x_ref, g1_ref, w_ref, qg_ref, kg_ref, bd_ref,
                q_ref, k_ref, v_ref, u_ref, kb_ref, vb_ref, km_ref, *, n_sub, kv_transposed):
    x = x_ref[...]
    ms = jnp.mean(x * x, axis=-1, keepdims=True)
    xn = (x * lax.rsqrt(ms + EPS)) * g1_ref[...]
    proj = jnp.dot(xn.astype(BF16), w_ref[...], preferred_element_type=F32)
    bd = bd_ref[...]

    def head_norm(t, g):
        hi, lo = _split_bf16(t * t)
        msq = jnp.dot(hi, bd, preferred_element_type=F32) + jnp.dot(lo, bd, preferred_element_type=F32)
        return (t * lax.rsqrt(msq + EPS)) * g

    q_ref[...] = head_norm(proj[:, :ATTN_W], qg_ref[...])
    k = head_norm(proj[:, ATTN_W:2 * ATTN_W], kg_ref[...])
    v = proj[:, 2 * ATTN_W:3 * ATTN_W]
    if kv_transposed:
        k_ref[0] = k.T
        v_ref[0] = v.T
    else:
        k_ref[...] = k
        v_ref[...] = v
    kb_ref[...] = k.astype(BF16)
    vb_ref[...] = v.astype(BF16)
    a = proj[:, 3 * ATTN_W:3 * ATTN_W + ATTN_W]
    gt = proj[:, 4 * ATTN_W:]
    u_ref[...] = a * jax.nn.sigmoid(gt)
    rows = k.shape[0] // n_sub
    for s in range(n_sub):
        km_ref[0, s:s + 1, :] = jnp.sum(k[s * rows:(s + 1) * rows], axis=0, keepdims=True) * (1.0 / rows)


def _front(x2d, g1, w_in_b, qg, kg, bd, tm, n_sub, seq=None):
    m, d = x2d.shape
    nw = w_in_b.shape[1]
    row = lambda i: (i, 0)
    fixed = lambda i: (0, 0)
    if seq is None:
        kv_shape = jax.ShapeDtypeStruct((m, ATTN_W), F32)
        kv_blk = pl.BlockSpec((tm, ATTN_W), row)
    else:
        tiles = seq // tm
        kv_shape = jax.ShapeDtypeStruct((m // seq, ATTN_W, seq), F32)
        kv_blk = pl.BlockSpec((1, ATTN_W, tm), lambda i: (i // tiles, 0, i % tiles))
    outs = (
        jax.ShapeDtypeStruct((m, ATTN_W), F32),
        kv_shape,
        kv_shape,
        jax.ShapeDtypeStruct((m, ATTN_W), F32),
        jax.ShapeDtypeStruct((m, ATTN_W), BF16),
        jax.ShapeDtypeStruct((m, ATTN_W), BF16),
        jax.ShapeDtypeStruct((m // tm, n_sub, ATTN_W), F32),
    )
    blk = pl.BlockSpec((tm, ATTN_W), row)
    return pl.pallas_call(
        functools.partial(_front_body, n_sub=n_sub, kv_transposed=seq is not None),
        grid=(m // tm,),
        in_specs=[pl.BlockSpec((tm, d), row), pl.BlockSpec((1, d), fixed),
                  pl.BlockSpec((d, nw), fixed), pl.BlockSpec((1, ATTN_W), fixed),
                  pl.BlockSpec((1, ATTN_W), fixed), pl.BlockSpec((ATTN_W, ATTN_W), fixed)],
        out_specs=(blk, kv_blk, kv_blk, blk, blk, blk,
                   pl.BlockSpec((1, n_sub, ATTN_W), lambda i: (i, 0, 0))),
        out_shape=outs,
        compiler_params=_cparams(("arbitrary",)),
        name="front",
    )(x2d, g1, w_in_b, qg, kg, bd)


HIST = 32


SUBLANES = 8


def _conv_body(u_ref, h0_ref, w_ref, b_ref, g_ref, o_ref, buf_ref, sh_ref, *, ts, chunk):
    t = pl.program_id(1)

    @pl.when(t == 0)
    def _():
        buf_ref[0:HIST, :] = h0_ref[0]

    @pl.when(t != 0)
    def _():
        buf_ref[0:HIST, :] = buf_ref[ts:ts + HIST, :]

    buf_ref[HIST:HIST + ts, :] = u_ref[0]
    span = sh_ref.shape[1]
    for r in range(1, SUBLANES):
        sh_ref[r - 1] = buf_ref[r:r + span, :]
    w = w_ref[...]
    off = HIST - (CONV_K - 1)
    for c in range(ts // chunk):
        acc = jnp.zeros((chunk, u_ref.shape[2]), F32)
        for j in range(CONV_K):
            a, r = divmod(off + j, SUBLANES)
            lo = c * chunk + a * SUBLANES
            rows = buf_ref[lo:lo + chunk, :] if r == 0 else sh_ref[r - 1, lo:lo + chunk, :]
            acc = acc + rows * w[j:j + 1, :]
        y = acc + b_ref[...]
        ms = jnp.mean(y * y, axis=-1, keepdims=True)
        y = (y * lax.rsqrt(ms + EPS)) * g_ref[...]
        o_ref[0, c * chunk:(c + 1) * chunk, :] = y * jax.nn.sigmoid(y)


def _conv(u3, hist0, conv_w, conv_b, conv_g, ts, chunk):
    b, l, c = u3.shape
    fixed = lambda i, t: (0, 0)
    return pl.pallas_call(
        functools.partial(_conv_body, ts=ts, chunk=chunk),
        grid=(b, l // ts),
        in_specs=[pl.BlockSpec((1, ts, c), lambda i, t: (i, t, 0)),
                  pl.BlockSpec((1, HIST, c), lambda i, t: (i, 0, 0)),
                  pl.BlockSpec((CONV_K, c), fixed), pl.BlockSpec((1, c), fixed),
                  pl.BlockSpec((1, c), fixed)],
        out_specs=pl.BlockSpec((1, ts, c), lambda i, t: (i, t, 0)),
        out_shape=jax.ShapeDtypeStruct((b, l, c), F32),
        scratch_shapes=[pltpu.VMEM((ts + HIST, c), F32),
                        pltpu.VMEM((SUBLANES - 1, ts + HIST - SUBLANES, c), F32)],
        compiler_params=_cparams(("arbitrary", "arbitrary")),
        name="conv_branch",
    )(u3, hist0, conv_w, conv_b, conv_g)


def _block_select(gate, blk_idx, n_valid, nb):
    height = gate.shape[0]
    valid = blk_idx < n_valid
    g = jnp.where(valid, gate, NEG)
    rank = jnp.zeros(gate.shape, jnp.int32)
    for s in range(1, nb):
        fwd = pltpu.roll(g, height - s, 0)
        bwd = pltpu.roll(g, nb - s, 0)
        wraps = blk_idx + s >= nb
        other = jnp.where(wraps, bwd, fwd)
        beats = (other > g) | ((other == g) & wraps)
        rank = rank + beats.astype(jnp.int32)
    return valid & (rank < MOBA_TOPK)


MASK_BIAS = -1e30
ATTN_GROUP = 4


def _attn_body(q_ref, kb_ref, vb_ref, km_ref, ind_ref, o_ref, s_ref, mx_ref, acc_ref, *, nb):
    i = pl.program_id(1)
    blk = MOBA_BLOCK
    hps = LANES // HEAD_DIM
    q = q_ref[0]
    km = km_ref[0]
    pad_rows = LANES - HEADS * nb
    kmt = jnp.concatenate([km] * HEADS + ([jnp.zeros((pad_rows, ATTN_W), F32)] if pad_rows else []), axis=0)
    r_head = lax.broadcasted_iota(jnp.int32, kmt.shape, 0) // nb
    c_head = lax.broadcasted_iota(jnp.int32, kmt.shape, 1) // HEAD_DIM
    kmt = jnp.where(r_head == c_head, kmt, 0.0)
    gate_t = _dot_nt3(kmt, q)
    rix = lax.broadcasted_iota(jnp.int32, (LANES, blk), 0)
    n_valid = jnp.where(rix < HEADS * nb, i, 0)
    sel_t = _block_select(gate_t, rix % nb, n_valid, nb)
    bias = jnp.where(sel_t, 0.0, MASK_BIAS).T
    lane = lax.broadcasted_iota(jnp.int32, (blk, LANES), 1)

    row = lax.broadcasted_iota(jnp.int32, (hps * blk, blk), 0) % blk
    col = lax.broadcasted_iota(jnp.int32, (hps * blk, blk), 1)
    causal = col <= row
    scale = HEAD_DIM ** -0.5
    own0 = pl.multiple_of(i * blk, blk)
    grp = ATTN_GROUP * blk
    n_grp = (i + ATTN_GROUP) // ATTN_GROUP

    def tilemax(s):
        out = s[:, :LANES]
        for t in range(1, s.shape[1] // LANES):
            out = jnp.maximum(out, s[:, t * LANES:(t + 1) * LANES])
        return out

    for c in range(ATTN_W // LANES):
        slab = slice(c * LANES, (c + 1) * LANES)
        qs = q[:, slab] * scale
        stacked = []
        for half in range(hps):
            h = c * hps + half
            qm = jnp.where((lane // HEAD_DIM) == half, qs, 0.0).astype(BF16)
            bh = jnp.where((lane // nb) == h, bias, 0.0).astype(BF16)
            stacked.append(jnp.concatenate([qm, bh], axis=1))
        q2 = jnp.concatenate(stacked, axis=0)

        mx_ref[...] = jnp.full(mx_ref.shape, NEG, F32)

        def scores(g, carry, q2=q2, slab=slab):
            g0 = pl.multiple_of(g * grp, grp)
            kj = jnp.concatenate([kb_ref[0, pl.ds(g0, grp), slab], ind_ref[pl.ds(g0, grp), :]], axis=1)
            s = _dot_nt(q2, kj)
            s_ref[:, pl.ds(g0, grp)] = s
            mx_ref[...] = jnp.maximum(mx_ref[...], tilemax(s))
            return carry

        lax.fori_loop(0, n_grp, scores, 0)
        s = jnp.where(causal, _dot_nt(q2[:, :LANES], kb_ref[0, pl.ds(own0, blk), slab]), NEG)
        s_ref[:, pl.ds(own0, blk)] = s
        m = jnp.max(jnp.maximum(mx_ref[...], tilemax(s)), axis=-1, keepdims=True)
        acc_ref[...] = jnp.zeros_like(acc_ref)

        def weigh(g, carry, m=m, slab=slab):
            g0 = pl.multiple_of(g * grp, grp)
            p = jnp.exp((s_ref[:, pl.ds(g0, grp)] - m).astype(BF16))
            vv = jnp.concatenate([vb_ref[0, pl.ds(g0, grp), slab], jnp.ones((grp, LANES), BF16)], axis=1)
            acc_ref[...] += jnp.dot(p, vv, preferred_element_type=F32)
            return carry

        lax.fori_loop(0, n_grp, weigh, 0)
        o = acc_ref[:, :LANES] / acc_ref[:, LANES:]
        out = o[0:blk]
        for half in range(1, hps):
            out = jnp.where((lane // HEAD_DIM) == half, o[half * blk:(half + 1) * blk], out)
        o_ref[0, :, slab] = out


def _moba_prompt(q3, kb3, vb3, km3):
    b, s, w = q3.shape
    nb = s // MOBA_BLOCK
    assert HEADS * nb <= LANES and nb % ATTN_GROUP == 0
    rows = (LANES // HEAD_DIM) * MOBA_BLOCK
    key_blk = lax.broadcasted_iota(jnp.int32, (s, LANES), 0) // MOBA_BLOCK
    lane = lax.broadcasted_iota(jnp.int32, (s, LANES), 1)
    ind = ((key_blk == lane % nb) & (lane < HEADS * nb)).astype(BF16)
    return pl.pallas_call(
        functools.partial(_attn_body, nb=nb),
        grid=(b, nb),
        in_specs=[pl.BlockSpec((1, MOBA_BLOCK, w), lambda bi, i: (bi, i, 0)),
                  pl.BlockSpec((1, s, w), lambda bi, i: (bi, 0, 0)),
                  pl.BlockSpec((1, s, w), lambda bi, i: (bi, 0, 0)),
                  pl.BlockSpec((1, nb, w), lambda bi, i: (bi, 0, 0)),
                  pl.BlockSpec((s, LANES), lambda bi, i: (0, 0))],
        out_specs=pl.BlockSpec((1, MOBA_BLOCK, w), lambda bi, i: (bi, i, 0)),
        out_shape=jax.ShapeDtypeStruct((b, s, w), F32),
        scratch_shapes=[pltpu.VMEM((rows, s), F32), pltpu.VMEM((rows, LANES), F32),
                        pltpu.VMEM((rows, 2 * LANES), F32)],
        compiler_params=_cparams(("arbitrary", "arbitrary")),
        name="moba_prompt",
    )(q3, kb3, vb3, km3, ind)


def _back_body(x_ref, a_ref, c_ref, w_ref, g2_ref, h_ref, hn_ref):
    mix = jnp.concatenate([a_ref[...], c_ref[...]], axis=-1).astype(BF16)
    h = x_ref[...] + jnp.dot(mix, w_ref[...], preferred_element_type=F32)
    h_ref[...] = h
    ms = jnp.mean(h * h, axis=-1, keepdims=True)
    hn_ref[...] = ((h * lax.rsqrt(ms + EPS)) * g2_ref[...]).astype(BF16)


def _back(x2d, attn2d, conv2d, w_out_b, g2, tm):
    m, d = x2d.shape
    row = lambda i: (i, 0)
    fixed = lambda i: (0, 0)
    return pl.pallas_call(
        _back_body,
        grid=(m // tm,),
        in_specs=[pl.BlockSpec((tm, d), row), pl.BlockSpec((tm, ATTN_W), row),
                  pl.BlockSpec((tm, d - ATTN_W), row), pl.BlockSpec((d, d), fixed),
                  pl.BlockSpec((1, d), fixed)],
        out_specs=(pl.BlockSpec((tm, d), row), pl.BlockSpec((tm, d), row)),
        out_shape=(jax.ShapeDtypeStruct((m, d), F32), jax.ShapeDtypeStruct((m, d), BF16)),
        compiler_params=_cparams(("arbitrary",)),
        name="back",
    )(x2d, attn2d, conv2d, w_out_b, g2)


def _pair_candidates():
    return [(r1, r2) for r1 in range(PEER_TOPK) for r2 in range(PEER_TOPK)
            if (r1 + 1) * (r2 + 1) <= PEER_TOPK]


RANK_MARK = 2.0 ** 100


def _top_rounds(s, k, exact):
    n = s.shape[0]
    idx = lax.broadcasted_iota(jnp.int32, s.shape, 0)
    vals = []
    for r in range(k):
        m = jnp.max(s, axis=0, keepdims=True)
        hit = s == m
        if exact:
            first = jnp.min(jnp.where(hit, idx, n), axis=0, keepdims=True)
            hit = idx == first
        s = jnp.where(hit, -(r + 1.0) * RANK_MARK, s)
        vals.append(m)
    rank = jnp.where(s <= -RANK_MARK, s * (-1.0 / RANK_MARK) - 1.0, float(k))
    count = jnp.sum((rank < k).astype(F32), axis=0, keepdims=True)
    return vals, rank, count


def _route_body(hn_ref, wq_ref, k1_ref, k2_ref, e1w_ref, cnt_ref, e2_ref, rk2_ref, qh_ref):
    tm = hn_ref.shape[0]
    qh_ref[...] = _dot_nt(wq_ref[...], hn_ref[...])
    pairs = _pair_candidates()
    n_pad = (-len(pairs)) % 8

    def route_head(h, exact):
        base = pl.multiple_of(h * (2 * HALF_KEY), 2 * HALF_KEY)
        q1h, q1l = _split_bf16(qh_ref[pl.ds(base, HALF_KEY), :])
        q2h, q2l = _split_bf16(qh_ref[pl.ds(base + HALF_KEY, HALF_KEY), :])
        k1h, k1l = _split_bf16(k1_ref[h])
        k2h, k2l = _split_bf16(k2_ref[h])
        dot = functools.partial(jnp.dot, preferred_element_type=F32)
        s1 = dot(k1h, q1h) + (dot(k1h, q1l) + dot(k1l, q1h))
        s2 = dot(k2h, q2h) + (dot(k2h, q2l) + dot(k2l, q2h))
        v1, rk1, n1 = _top_rounds(s1, PEER_TOPK, exact)
        v2, rk2, n2 = _top_rounds(s2, PEER_TOPK, exact)
        cand = jnp.concatenate([v1[r1] + v2[r2] for r1, r2 in pairs]
                               + [jnp.full((n_pad, tm), NEG, F32)], axis=0)
        _, rkc, nc = _top_rounds(cand, PEER_TOPK, exact)
        picked = (rkc < PEER_TOPK).astype(F32)
        ez = jnp.exp(cand - cand[0:1, :]) * picked
        z = jnp.sum(ez, axis=0, keepdims=True)
        cnt = jnp.zeros((N_KEYS, tm), F32)
        for r1 in range(PEER_TOPK):
            rows = [i for i, (a, _) in enumerate(pairs) if a == r1]
            c_r1 = jnp.sum(picked[rows[0]:rows[-1] + 1, :], axis=0, keepdims=True)
            cnt = jnp.where(rk1 == r1, c_r1, cnt)
        e1w_ref[h] = jnp.exp(s1 - v1[0]) / z
        cnt_ref[h] = cnt
        e2_ref[h] = jnp.exp(s2 - v2[0]).astype(BF16)
        rk2_ref[h] = rk2.astype(F32).astype(BF16)
        return jnp.max(jnp.abs(n1 - PEER_TOPK) + jnp.abs(n2 - PEER_TOPK) + jnp.abs(nc - PEER_TOPK))

    def heads(i, carry):
        hs = [i * ROUTE_UNROLL + k for k in range(ROUTE_UNROLL)]
        n_tied = [route_head(h, exact=False) for h in hs]
        for h, n in zip(hs, n_tied):
            @pl.when(n > 0.0)
            def _(h=h):
                route_head(h, exact=True)

        return carry

    lax.fori_loop(0, PEER_HEADS // ROUTE_UNROLL, heads, 0)


def _route(hn, wq_t, keys1, keys2, tm):
    m, d = hn.shape
    nq = wq_t.shape[0]
    out = jax.ShapeDtypeStruct((PEER_HEADS, N_KEYS, m), F32)
    outb = jax.ShapeDtypeStruct((PEER_HEADS, N_KEYS, m), BF16)
    oblk = pl.BlockSpec((PEER_HEADS, N_KEYS, tm), lambda i: (0, 0, i))
    kblk = pl.BlockSpec((PEER_HEADS, N_KEYS, HALF_KEY), lambda i: (0, 0, 0))
    return pl.pallas_call(
        _route_body,
        grid=(m // tm,),
        in_specs=[pl.BlockSpec((tm, d), lambda i: (i, 0)), pl.BlockSpec((nq, d), lambda i: (0, 0)),
                  kblk, kblk],
        out_specs=(oblk, oblk, oblk, oblk),
        out_shape=(out, out, outb, outb),
        scratch_shapes=[pltpu.VMEM((nq, tm), F32)],
        compiler_params=_cparams(("arbitrary",)),
        name="peer_route",
    )(hn, wq_t, keys1, keys2)


def _experts_body(hn_ref, u_ref, vt_ref, e1w_ref, cnt_ref, e2_ref, rk2_ref, h_ref, y_ref, acc_ref, g_ref, *, n_a):
    e = pl.program_id(1)
    n_e = pl.num_programs(1)
    tm = hn_ref.shape[0]
    half_a = n_a // 2

    def build(tile, key1s, chained=True):
        zero = jnp.zeros((), BF16)
        tiles = []
        for al in key1s:
            a = tile * n_a + al
            c16 = [jnp.broadcast_to(cnt_ref[h, pl.ds(a, 1), :], (BF16_ROWS, tm)).astype(BF16)
                   for h in range(PEER_HEADS)]
            w16 = [jnp.broadcast_to(e1w_ref[h, pl.ds(a, 1), :], (BF16_ROWS, tm)).astype(BF16)
                   for h in range(PEER_HEADS)]
            for r in range(0, N_KEYS, BF16_ROWS):
                g = None
                for h in range(PEER_HEADS):
                    t = jnp.where(rk2_ref[h, r:r + BF16_ROWS, :] < c16[h],
                                  e2_ref[h, r:r + BF16_ROWS, :], zero) * w16[h]
                    g = t if g is None else g + t
                tiles.append(g)
                if not chained:
                    continue
                sixteen = jnp.uint32(16)
                bits = pltpu.bitcast(g, jnp.uint32)
                bits = lax.shift_right_logical(lax.shift_right_logical(bits, sixteen), sixteen)
                if len(tiles) % WEIGHT_HOP == 0:
                    bits = pltpu.roll(bits, 1, 1)
                zero = pltpu.bitcast(bits, BF16)
        return jnp.concatenate(tiles, axis=0)

    @pl.when(e == 0)
    def _():
        acc_ref[...] = jnp.zeros_like(acc_ref)
        g_ref[0] = build(0, range(half_a, n_a), chained=False)

    cur = e % 2
    pre = _dot_nt(u_ref[...], hn_ref[...])
    g_own = build(e, range(0, half_a))
    act = (pre * 0.5 * (1.0 + lax.erf(pre * (2.0 ** -0.5)))).astype(BF16)
    p = jnp.concatenate([g_own, g_ref[cur]], axis=0) * act
    acc_ref[...] += jnp.dot(vt_ref[...], p, preferred_element_type=F32)
    g_ref[1 - cur] = build(jnp.minimum(e + 1, n_e - 1), range(half_a, n_a))

    @pl.when(e == n_e - 1)
    def _():
        y_ref[...] = h_ref[...] + acc_ref[...].T


def _experts(hn, u_b, vt_b, e1w, cnt, e2, rk2, h, tm, te):
    m, d = hn.shape
    ne = u_b.shape[0]
    rblk = pl.BlockSpec((PEER_HEADS, N_KEYS, tm), lambda i, e: (0, 0, i))
    return pl.pallas_call(
        functools.partial(_experts_body, n_a=te // N_KEYS),
        grid=(m // tm, ne // te),
        in_specs=[pl.BlockSpec((tm, d), lambda i, e: (i, 0)),
                  pl.BlockSpec((te, d), lambda i, e: (e, 0)),
                  pl.BlockSpec((d, te), lambda i, e: (0, e)),
                  rblk, rblk, rblk, rblk,
                  pl.BlockSpec((tm, d), lambda i, e: (i, 0))],
        out_specs=pl.BlockSpec((tm, d), lambda i, e: (i, 0)),
        out_shape=jax.ShapeDtypeStruct((m, d), F32),
        scratch_shapes=[pltpu.VMEM((d, tm), F32), pltpu.VMEM((2, te // 2, tm), BF16)],
        compiler_params=_cparams(("arbitrary", "arbitrary")),
        name="peer_experts",
    )(hn, u_b, vt_b, e1w, cnt, e2, rk2, h)


PAGES_PER_BLOCK = MOBA_BLOCK // PAGE
SAMPLE_GROUP = 8


def _dot_nn3(a, b):
    ah, al = _split_bf16(a)
    bh, bl = _split_bf16(b)
    dot = functools.partial(jnp.dot, preferred_element_type=F32)
    return dot(ah, bh) + (dot(ah, bl) + dot(al, bh))


def _sample_body(pt_ref, q_ref, kn_ref, vn_ref, *refs, nbp):
    n_pg = SAMPLE_GROUP * PAGES_PER_BLOCK
    k_pages = refs[:n_pg]
    v_pages = refs[n_pg:2 * n_pg]
    o_ref, km_s, m_s, acc_s = refs[2 * n_pg:]
    j = pl.program_id(1)
    dt = q_ref.shape[2]
    grp = SAMPLE_GROUP
    keys = grp * MOBA_BLOCK
    scale = HEAD_DIM ** -0.5
    lane_d = lax.broadcasted_iota(jnp.int32, (HEAD_DIM, LANES), 1)
    lane_t = lax.broadcasted_iota(jnp.int32, (dt, LANES), 1)
    own_blk = (lax.broadcasted_iota(jnp.int32, (grp, keys), 1) // MOBA_BLOCK
               == lax.broadcasted_iota(jnp.int32, (grp, keys), 0))
    ones = jnp.ones((HEAD_DIM, keys), BF16)

    @pl.when(j == 0)
    def _():
        km_s[...] = jnp.zeros_like(km_s)
        m_s[...] = jnp.zeros_like(m_s)

    for h in range(HEADS):
        kt = jnp.concatenate([r[h] for r in k_pages], axis=1)
        vt = jnp.concatenate([r[h] for r in v_pages], axis=1)
        qs = (q_ref[0, h] * scale).astype(BF16)
        s = jnp.dot(qs, kt.astype(BF16), preferred_element_type=F32)
        m = jnp.max(s, axis=-1, keepdims=True)
        p = jnp.exp(s - m)
        p_rows = jnp.concatenate(
            [jnp.where(own_blk, jnp.broadcast_to(p[t:t + 1], (grp, keys)), 0.0) for t in range(dt)],
            axis=0).astype(BF16)
        acc = _dot_nt(p_rows, jnp.concatenate([vt.astype(BF16), ones], axis=0))
        g0 = pl.multiple_of(j * grp, grp)
        for t in range(dt):
            acc_s[h, t, pl.ds(g0, grp), :] = acc[t * grp:(t + 1) * grp]
        m_s[h] = jnp.where(lane_t // grp == j, m, m_s[h])
        km = km_s[h]
        for g in range(grp):
            kmean = jnp.sum(kt[:, g * MOBA_BLOCK:(g + 1) * MOBA_BLOCK], axis=1, keepdims=True)
            km = jnp.where(lane_d == j * grp + g, kmean * (1.0 / MOBA_BLOCK), km)
        km_s[h] = km

    @pl.when(j == pl.num_programs(1) - 1)
    def _():
        tri = (lax.broadcasted_iota(jnp.int32, (dt, dt), 1) <= lax.broadcasted_iota(jnp.int32, (dt, dt), 0))
        own_tok = lax.broadcasted_iota(jnp.int32, (dt, dt * nbp), 1) // nbp == lax.broadcasted_iota(
            jnp.int32, (dt, dt * nbp), 0)
        for h in range(HEADS):
            q = q_ref[0, h]
            gate = jnp.where(lane_t < nbp, _dot_nn3(q, km_s[h]), NEG)
            picked = jnp.zeros(gate.shape, jnp.bool_)
            for _ in range(MOBA_TOPK):
                top = jnp.max(gate, axis=-1, keepdims=True)
                first = jnp.min(jnp.where(gate == top, lane_t, LANES), axis=-1, keepdims=True)
                hit = lane_t == first
                picked = picked | hit
                gate = jnp.where(hit, NEG, gate)
            qs = (q * scale).astype(BF16)
            sn = jnp.where(tri, _dot_nt(qs, kn_ref[0, h].astype(BF16)), NEG)
            mn = jnp.max(sn, axis=-1, keepdims=True)
            pn = jnp.exp(sn - mn)
            ln = jnp.sum(pn, axis=-1, keepdims=True)
            an = jnp.dot(pn.astype(BF16), vn_ref[0, h].astype(BF16), preferred_element_type=F32)
            m_all = m_s[h]
            m_tot = jnp.maximum(jnp.max(jnp.where(picked, m_all, NEG), axis=-1, keepdims=True), mn)
            w = jnp.where(picked, jnp.exp(m_all - m_tot), 0.0)
            wn = jnp.exp(mn - m_tot)
            w_rows = jnp.where(own_tok, jnp.concatenate([w[:, :nbp]] * dt, axis=1), 0.0)
            past = _dot_nn3(w_rows, acc_s[h].reshape(dt * nbp, 2 * HEAD_DIM))
            den = past[:, HEAD_DIM:HEAD_DIM + 1] + wn * ln
            o_ref[0, h] = (past[:, :HEAD_DIM] + wn * an) / den


def _moba_sample(q4, kn4, vn4, cache_k, cache_v, page_table, layer):
    db, _, dt, _ = q4.shape
    nbp = (page_table.shape[1] * PAGE) // MOBA_BLOCK
    assert MOBA_TOPK <= nbp <= LANES and nbp % SAMPLE_GROUP == 0
    kt5 = jnp.transpose(cache_k, (0, 1, 3, 4, 2))
    vt5 = jnp.transpose(cache_v, (0, 1, 3, 4, 2))
    new_blk = pl.BlockSpec((1, HEADS, dt, HEAD_DIM), lambda b, j, pt: (b, 0, 0, 0))
    n_pg = SAMPLE_GROUP * PAGES_PER_BLOCK

    def page_spec(r):
        return pl.BlockSpec((None, None, HEADS, HEAD_DIM, PAGE),
                            lambda b, j, pt: (layer, pt[b, n_pg * j + r], 0, 0, 0))

    pages = [page_spec(r) for r in range(n_pg)]
    return pl.pallas_call(
        functools.partial(_sample_body, nbp=nbp),
        grid_spec=pltpu.PrefetchScalarGridSpec(
            num_scalar_prefetch=1,
            grid=(db, nbp // SAMPLE_GROUP),
            in_specs=[new_blk, new_blk, new_blk] + pages + pages,
            out_specs=new_blk,
            scratch_shapes=[pltpu.VMEM((HEADS, HEAD_DIM, LANES), F32), pltpu.VMEM((HEADS, dt, LANES), F32),
                            pltpu.VMEM((HEADS, dt, nbp, 2 * HEAD_DIM), F32)]),
        out_shape=jax.ShapeDtypeStruct((db, HEADS, dt, HEAD_DIM), F32),
        compiler_params=_cparams(("arbitrary", "arbitrary")),
        name="moba_sample",
    )(page_table, q4, kn4, vn4, *([kt5] * n_pg), *([vt5] * n_pg))


def _mixer_tail(x2d, attn2d, conv2d, w_out_b, g2, wq_t, keys1, keys2, u_b, vt_b, tm, tr, tx, te):
    h, hn = _back(x2d, attn2d, conv2d, w_out_b, g2, tm)
    routing = _route(hn, wq_t, keys1, keys2, tr)
    return _experts(hn, u_b, vt_b, *routing, h, tx, te)


TILE_M = 512
TILE_ROUTE = 256
TILE_X = 512
TILE_E = 2048
CONV_CHUNK = 64


def kernel(x_prompt, x_sample, cache_k, cache_v, state_conv, page_table, norm1_g, w_in, q_norm_g, k_norm_g, conv_w, conv_b, conv_norm_g, w_out, norm2_g, peer_w_query, peer_keys1, peer_keys2, peer_u, peer_v):
    depth = w_in.shape[0]
    bsz, seq, d = x_prompt.shape
    db, dt, _ = x_sample.shape
    assert seq % TILE_M == 0 and TILE_M % MOBA_BLOCK == 0 and d - ATTN_W == ATTN_W
    assert (page_table.shape[1] * PAGE) % MOBA_BLOCK == 0, "partially filled own block is not supported"
    bd = _head_blockdiag()
    xp, xs = x_prompt, x_sample
    rows = {n: [] for n in ("kp", "vp", "cp", "ks", "vs", "cs")}
    for l in range(depth):
        w_in_b = w_in[l].astype(BF16)
        w_out_b = w_out[l].astype(BF16)
        wq_t = peer_w_query[l].T.astype(BF16)
        u_b = peer_u[l].astype(BF16)
        vt_b = peer_v[l].T.astype(BF16)
        g1 = norm1_g[l][None]
        g2 = norm2_g[l][None]
        qg = q_norm_g[l].reshape(1, ATTN_W)
        kg = k_norm_g[l].reshape(1, ATTN_W)
        cb = conv_b[l][None]
        cg = conv_norm_g[l][None]
        tail = functools.partial(_mixer_tail, w_out_b=w_out_b, g2=g2, wq_t=wq_t, keys1=peer_keys1[l],
                                 keys2=peer_keys2[l], u_b=u_b, vt_b=vt_b, te=TILE_E)

        x2d = xp.reshape(bsz * seq, d)
        q, k_t, v_t, u, kb, vb, km = _front(x2d, g1, w_in_b, qg, kg, bd, TILE_M, TILE_M // MOBA_BLOCK, seq=seq)
        r3 = lambda t: t.reshape(bsz, seq, ATTN_W)
        from_t = lambda t: t.reshape(bsz, HEADS, HEAD_DIM, seq).transpose(0, 3, 1, 2)
        attn = _moba_prompt(r3(q), r3(kb), r3(vb), km.reshape(bsz, seq // MOBA_BLOCK, ATTN_W))
        conv = _conv(r3(u), jnp.zeros((bsz, HIST, ATTN_W), F32), conv_w[l], cb, cg, TILE_M, CONV_CHUNK)
        y = tail(x2d, attn.reshape(-1, ATTN_W), conv.reshape(-1, ATTN_W),
                 tm=TILE_M, tr=TILE_ROUTE, tx=TILE_X)
        xp = y.reshape(bsz, seq, d)
        rows["kp"].append(from_t(k_t))
        rows["vp"].append(from_t(v_t))
        rows["cp"].append(r3(u)[:, -(CONV_K - 1):])

        n_s = db * dt
        x2d = xs.reshape(n_s, d)
        q, k, v, u, _, _, _ = _front(x2d, g1, w_in_b, qg, kg, bd, n_s, 1)
        h4 = lambda t: t.reshape(db, dt, HEADS, HEAD_DIM)
        hm = lambda t: h4(t).transpose(0, 2, 1, 3)
        attn = _moba_sample(hm(q), hm(k), hm(v), cache_k, cache_v, page_table, l).transpose(0, 2, 1, 3)
        u3 = u.reshape(db, dt, ATTN_W)
        hist = jnp.concatenate([jnp.zeros((db, HIST - (CONV_K - 1), ATTN_W), F32), state_conv[l]], axis=1)
        conv = _conv(u3, hist, conv_w[l], cb, cg, dt, dt)
        y = tail(x2d, attn.reshape(n_s, ATTN_W), conv.reshape(n_s, ATTN_W), tm=n_s, tr=n_s, tx=n_s)
        xs = y.reshape(db, dt, d)
        rows["ks"].append(h4(k))
        rows["vs"].append(h4(v))
        rows["cs"].append(jnp.concatenate([state_conv[l], u3], axis=1)[:, -(CONV_K - 1):])
    return (xp, xs, jnp.stack(rows["kp"]), jnp.stack(rows["vp"]), jnp.stack(rows["cp"]),
            jnp.stack(rows["ks"]), jnp.stack(rows["vs"]), jnp.stack(rows["cs"]))
```

```python
import functools

import jax
import jax.numpy as jnp
from jax import lax
from jax.experimental import pallas as pl
from jax.experimental.pallas import tpu as pltpu

F32 = jnp.float32
BF16 = jnp.bfloat16

HEADS = 8
HEAD_DIM = 64
ATTN_W = HEADS * HEAD_DIM
CONV_K = 31
MOBA_BLOCK = 256
MOBA_TOPK = 3
PAGE = 128
PEER_HEADS = 8
N_KEYS = 128
PEER_TOPK = 16
HALF_KEY = 128
EPS = 1e-6
NEG = float("-inf")

LANES = 128
BF16_ROWS = 16
ROUTE_UNROLL = 4
WEIGHT_HOP = 4
VMEM_LIMIT = 56 * 1024 * 1024


def _cparams(sem):
    return pltpu.CompilerParams(dimension_semantics=sem, vmem_limit_bytes=VMEM_LIMIT)


def _split_bf16(x):
    hi = x.astype(BF16)
    lo = (x - hi.astype(F32)).astype(BF16)
    return hi, lo


def _dot_nt(a, b):
    return lax.dot_general(a, b, (((1,), (1,)), ((), ())), preferred_element_type=F32)


def _dot_nt3(a, b):
    ah, al = _split_bf16(a)
    bh, bl = _split_bf16(b)
    return _dot_nt(ah, bh) + (_dot_nt(ah, bl) + _dot_nt(al, bh))


def _head_blockdiag():
    r = lax.broadcasted_iota(jnp.int32, (ATTN_W, ATTN_W), 0) // HEAD_DIM
    c = lax.broadcasted_iota(jnp.int32, (ATTN_W, ATTN_W), 1) // HEAD_DIM
    return jnp.where(r == c, 1.0 / HEAD_DIM, 0.0).astype(BF16)


def _front_body(x_ref, g1_ref, w_ref, qg_ref, kg_ref, bd_ref,
                q_ref, k_ref, v_ref, u_ref, kb_ref, vb_ref, km_ref, *, n_sub, kv_transposed):
    x = x_ref[...]
    ms = jnp.mean(x * x, axis=-1, keepdims=True)
    xn = (x * lax.rsqrt(ms + EPS)) * g1_ref[...]
    proj = jnp.dot(xn.astype(BF16), w_ref[...], preferred_element_type=F32)
    bd = bd_ref[...]

    def head_norm(t, g):
        hi, lo = _split_bf16(t * t)
        msq = jnp.dot(hi, bd, preferred_element_type=F32) + jnp.dot(lo, bd, preferred_element_type=F32)
        return (t * lax.rsqrt(msq + EPS)) * g

    q_ref[...] = head_norm(proj[:, :ATTN_W], qg_ref[...])
    k = head_norm(proj[:, ATTN_W:2 * ATTN_W], kg_ref[...])
    v = proj[:, 2 * ATTN_W:3 * ATTN_W]
    if kv_transposed:
        k_ref[0] = k.T
        v_ref[0] = v.T
    else:
        k_ref[...] = k
        v_ref[...] = v
    kb_ref[...] = k.astype(BF16)
    vb_ref[...] = v.astype(BF16)
    a = proj[:, 3 * ATTN_W:3 * ATTN_W + ATTN_W]
    gt = proj[:, 4 * ATTN_W:]
    u_ref[...] = a * jax.nn.sigmoid(gt)
    rows = k.shape[0] // n_sub
    for s in range(n_sub):
        km_ref[0, s:s + 1, :] = jnp.sum(k[s * rows:(s + 1) * rows], axis=0, keepdims=True) * (1.0 / rows)


def _front(x2d, g1, w_in_b, qg, kg, bd, tm, n_sub, seq=None):
    m, d = x2d.shape
    nw = w_in_b.shape[1]
    row = lambda i: (i, 0)
    fixed = lambda i: (0, 0)
    if seq is None:
        kv_shape = jax.ShapeDtypeStruct((m, ATTN_W), F32)
        kv_blk = pl.BlockSpec((tm, ATTN_W), row)
    else:
        tiles = seq // tm
        kv_shape = jax.ShapeDtypeStruct((m // seq, ATTN_W, seq), F32)
        kv_blk = pl.BlockSpec((1, ATTN_W, tm), lambda i: (i // tiles, 0, i % tiles))
    outs = (
        jax.ShapeDtypeStruct((m, ATTN_W), F32),
        kv_shape,
        kv_shape,
        jax.ShapeDtypeStruct((m, ATTN_W), F32),
        jax.ShapeDtypeStruct((m, ATTN_W), BF16),
        jax.ShapeDtypeStruct((m, ATTN_W), BF16),
        jax.ShapeDtypeStruct((m // tm, n_sub, ATTN_W), F32),
    )
    blk = pl.BlockSpec((tm, ATTN_W), row)
    return pl.pallas_call(
        functools.partial(_front_body, n_sub=n_sub, kv_transposed=seq is not None),
        grid=(m // tm,),
        in_specs=[pl.BlockSpec((tm, d), row), pl.BlockSpec((1, d), fixed),
                  pl.BlockSpec((d, nw), fixed), pl.BlockSpec((1, ATTN_W), fixed),
                  pl.BlockSpec((1, ATTN_W), fixed), pl.BlockSpec((ATTN_W, ATTN_W), fixed)],
        out_specs=(blk, kv_blk, kv_blk, blk, blk, blk,
                   pl.BlockSpec((1, n_sub, ATTN_W), lambda i: (i, 0, 0))),
        out_shape=outs,
        compiler_params=_cparams(("arbitrary",)),
        name="front",
    )(x2d, g1, w_in_b, qg, kg, bd)


HIST = 32


SUBLANES = 8


def _conv_body(u_ref, h0_ref, w_ref, b_ref, g_ref, o_ref, buf_ref, sh_ref, *, ts, chunk):
    t = pl.program_id(1)

    @pl.when(t == 0)
    def _():
        buf_ref[0:HIST, :] = h0_ref[0]

    @pl.when(t != 0)
    def _():
        buf_ref[0:HIST, :] = buf_ref[ts:ts + HIST, :]

    buf_ref[HIST:HIST + ts, :] = u_ref[0]
    span = sh_ref.shape[1]
    for r in range(1, SUBLANES):
        sh_ref[r - 1] = buf_ref[r:r + span, :]
    w = w_ref[...]
    off = HIST - (CONV_K - 1)
    for c in range(ts // chunk):
        acc = jnp.zeros((chunk, u_ref.shape[2]), F32)
        for j in range(CONV_K):
            a, r = divmod(off + j, SUBLANES)
            lo = c * chunk + a * SUBLANES
            rows = buf_ref[lo:lo + chunk, :] if r == 0 else sh_ref[r - 1, lo:lo + chunk, :]
            acc = acc + rows * w[j:j + 1, :]
        y = acc + b_ref[...]
        ms = jnp.mean(y * y, axis=-1, keepdims=True)
        y = (y * lax.rsqrt(ms + EPS)) * g_ref[...]
        o_ref[0, c * chunk:(c + 1) * chunk, :] = y * jax.nn.sigmoid(y)


def _conv(u3, hist0, conv_w, conv_b, conv_g, ts, chunk):
    b, l, c = u3.shape
    fixed = lambda i, t: (0, 0)
    return pl.pallas_call(
        functools.partial(_conv_body, ts=ts, chunk=chunk),
        grid=(b, l // ts),
        in_specs=[pl.BlockSpec((1, ts, c), lambda i, t: (i, t, 0)),
                  pl.BlockSpec((1, HIST, c), lambda i, t: (i, 0, 0)),
                  pl.BlockSpec((CONV_K, c), fixed), pl.BlockSpec((1, c), fixed),
                  pl.BlockSpec((1, c), fixed)],
        out_specs=pl.BlockSpec((1, ts, c), lambda i, t: (i, t, 0)),
        out_shape=jax.ShapeDtypeStruct((b, l, c), F32),
        scratch_shapes=[pltpu.VMEM((ts + HIST, c), F32),
                        pltpu.VMEM((SUBLANES - 1, ts + HIST - SUBLANES, c), F32)],
        compiler_params=_cparams(("arbitrary", "arbitrary")),
        name="conv_branch",
    )(u3, hist0, conv_w, conv_b, conv_g)


def _block_select(gate, blk_idx, n_valid, nb):
    height = gate.shape[0]
    valid = blk_idx < n_valid
    g = jnp.where(valid, gate, NEG)
    rank = jnp.zeros(gate.shape, jnp.int32)
    for s in range(1, nb):
        fwd = pltpu.roll(g, height - s, 0)
        bwd = pltpu.roll(g, nb - s, 0)
        wraps = blk_idx + s >= nb
        other = jnp.where(wraps, bwd, fwd)
        beats = (other > g) | ((other == g) & wraps)
        rank = rank + beats.astype(jnp.int32)
    return valid & (rank < MOBA_TOPK)


MASK_BIAS = -1e30
ATTN_GROUP = 4
ATTN_SLABS = 2


def _attn_body(q_ref, kb_ref, vb_ref, km_ref, ind_ref, o_ref, s_ref, mx_ref, acc_ref, *, nb):
    i = pl.program_id(1)
    blk = MOBA_BLOCK
    hps = LANES // HEAD_DIM
    q = q_ref[0]
    km = km_ref[0]
    pad_rows = LANES - HEADS * nb
    kmt = jnp.concatenate([km] * HEADS + ([jnp.zeros((pad_rows, ATTN_W), F32)] if pad_rows else []), axis=0)
    r_head = lax.broadcasted_iota(jnp.int32, kmt.shape, 0) // nb
    c_head = lax.broadcasted_iota(jnp.int32, kmt.shape, 1) // HEAD_DIM
    kmt = jnp.where(r_head == c_head, kmt, 0.0)
    gate_t = _dot_nt3(kmt, q)
    rix = lax.broadcasted_iota(jnp.int32, (LANES, blk), 0)
    n_valid = jnp.where(rix < HEADS * nb, i, 0)
    sel_t = _block_select(gate_t, rix % nb, n_valid, nb)
    bias = jnp.where(sel_t, 0.0, MASK_BIAS).T
    lane = lax.broadcasted_iota(jnp.int32, (blk, LANES), 1)

    row = lax.broadcasted_iota(jnp.int32, (hps * blk, blk), 0) % blk
    col = lax.broadcasted_iota(jnp.int32, (hps * blk, blk), 1)
    causal = col <= row
    scale = HEAD_DIM ** -0.5
    own0 = pl.multiple_of(i * blk, blk)
    grp = ATTN_GROUP * blk
    n_grp = (i + ATTN_GROUP) // ATTN_GROUP

    def tilemax(s):
        out = s[:, :LANES]
        for t in range(1, s.shape[1] // LANES):
            out = jnp.maximum(out, s[:, t * LANES:(t + 1) * LANES])
        return out

    def stacked_queries(c):
        qs = q[:, c * LANES:(c + 1) * LANES] * scale
        stacked = []
        for half in range(hps):
            h = c * hps + half
            qm = jnp.where((lane // HEAD_DIM) == half, qs, 0.0).astype(BF16)
            bh = jnp.where((lane // nb) == h, bias, 0.0).astype(BF16)
            stacked.append(jnp.concatenate([qm, bh], axis=1))
        return jnp.concatenate(stacked, axis=0)

    for c0 in range(0, ATTN_W // LANES, ATTN_SLABS):
        slabs = [slice((c0 + k) * LANES, (c0 + k + 1) * LANES) for k in range(ATTN_SLABS)]
        q2s = [stacked_queries(c0 + k) for k in range(ATTN_SLABS)]

        mx_ref[...] = jnp.full(mx_ref.shape, NEG, F32)

        def scores(g, carry, q2s=q2s, slabs=slabs):
            g0 = pl.multiple_of(g * grp, grp)
            ind = ind_ref[pl.ds(g0, grp), :]
            for k in range(ATTN_SLABS):
                kj = jnp.concatenate([kb_ref[0, pl.ds(g0, grp), slabs[k]], ind], axis=1)
                s = _dot_nt(q2s[k], kj)
                s_ref[k, :, pl.ds(g0, grp)] = s
                mx_ref[k] = jnp.maximum(mx_ref[k], tilemax(s))
            return carry

        lax.fori_loop(0, n_grp, scores, 0)
        ms = []
        for k in range(ATTN_SLABS):
            s = jnp.where(causal, _dot_nt(q2s[k][:, :LANES], kb_ref[0, pl.ds(own0, blk), slabs[k]]), NEG)
            s_ref[k, :, pl.ds(own0, blk)] = s
            ms.append(jnp.max(jnp.maximum(mx_ref[k], tilemax(s)), axis=-1, keepdims=True))
        acc_ref[...] = jnp.zeros_like(acc_ref)

        def weigh(g, carry, ms=ms, slabs=slabs):
            g0 = pl.multiple_of(g * grp, grp)
            ones = jnp.ones((grp, LANES), BF16)
            for k in range(ATTN_SLABS):
                p = jnp.exp((s_ref[k, :, pl.ds(g0, grp)] - ms[k]).astype(BF16))
                vv = jnp.concatenate([vb_ref[0, pl.ds(g0, grp), slabs[k]], ones], axis=1)
                acc_ref[k] += jnp.dot(p, vv, preferred_element_type=F32)
            return carry

        lax.fori_loop(0, n_grp, weigh, 0)
        for k in range(ATTN_SLABS):
            o = acc_ref[k, :, :LANES] / acc_ref[k, :, LANES:]
            out = o[0:blk]
            for half in range(1, hps):
                out = jnp.where((lane // HEAD_DIM) == half, o[half * blk:(half + 1) * blk], out)
            o_ref[0, :, slabs[k]] = out


def _moba_prompt(q3, kb3, vb3, km3):
    b, s, w = q3.shape
    nb = s // MOBA_BLOCK
    assert HEADS * nb <= LANES and nb % ATTN_GROUP == 0
    rows = (LANES // HEAD_DIM) * MOBA_BLOCK
    key_blk = lax.broadcasted_iota(jnp.int32, (s, LANES), 0) // MOBA_BLOCK
    lane = lax.broadcasted_iota(jnp.int32, (s, LANES), 1)
    ind = ((key_blk == lane % nb) & (lane < HEADS * nb)).astype(BF16)
    return pl.pallas_call(
        functools.partial(_attn_body, nb=nb),
        grid=(b, nb),
        in_specs=[pl.BlockSpec((1, MOBA_BLOCK, w), lambda bi, i: (bi, i, 0)),
                  pl.BlockSpec((1, s, w), lambda bi, i: (bi, 0, 0)),
                  pl.BlockSpec((1, s, w), lambda bi, i: (bi, 0, 0)),
                  pl.BlockSpec((1, nb, w), lambda bi, i: (bi, 0, 0)),
                  pl.BlockSpec((s, LANES), lambda bi, i: (0, 0))],
        out_specs=pl.BlockSpec((1, MOBA_BLOCK, w), lambda bi, i: (bi, i, 0)),
        out_shape=jax.ShapeDtypeStruct((b, s, w), F32),
        scratch_shapes=[pltpu.VMEM((ATTN_SLABS, rows, s), F32), pltpu.VMEM((ATTN_SLABS, rows, LANES), F32),
                        pltpu.VMEM((ATTN_SLABS, rows, 2 * LANES), F32)],
        compiler_params=_cparams(("arbitrary", "arbitrary")),
        name="moba_prompt",
    )(q3, kb3, vb3, km3, ind)


def _back_body(x_ref, a_ref, c_ref, w_ref, g2_ref, h_ref, hn_ref):
    mix = jnp.concatenate([a_ref[...], c_ref[...]], axis=-1).astype(BF16)
    h = x_ref[...] + jnp.dot(mix, w_ref[...], preferred_element_type=F32)
    h_ref[...] = h
    ms = jnp.mean(h * h, axis=-1, keepdims=True)
    hn_ref[...] = ((h * lax.rsqrt(ms + EPS)) * g2_ref[...]).astype(BF16)


def _back(x2d, attn2d, conv2d, w_out_b, g2, tm):
    m, d = x2d.shape
    row = lambda i: (i, 0)
    fixed = lambda i: (0, 0)
    return pl.pallas_call(
        _back_body,
        grid=(m // tm,),
        in_specs=[pl.BlockSpec((tm, d), row), pl.BlockSpec((tm, ATTN_W), row),
                  pl.BlockSpec((tm, d - ATTN_W), row), pl.BlockSpec((d, d), fixed),
                  pl.BlockSpec((1, d), fixed)],
        out_specs=(pl.BlockSpec((tm, d), row), pl.BlockSpec((tm, d), row)),
        out_shape=(jax.ShapeDtypeStruct((m, d), F32), jax.ShapeDtypeStruct((m, d), BF16)),
        compiler_params=_cparams(("arbitrary",)),
        name="back",
    )(x2d, attn2d, conv2d, w_out_b, g2)


def _pair_candidates():
    return [(r1, r2) for r1 in range(PEER_TOPK) for r2 in range(PEER_TOPK)
            if (r1 + 1) * (r2 + 1) <= PEER_TOPK]


RANK_MARK = 2.0 ** 100


def _top_rounds(s, k, exact):
    n = s.shape[0]
    idx = lax.broadcasted_iota(jnp.int32, s.shape, 0)
    vals = []
    for r in range(k):
        m = jnp.max(s, axis=0, keepdims=True)
        hit = s == m
        if exact:
            first = jnp.min(jnp.where(hit, idx, n), axis=0, keepdims=True)
            hit = idx == first
        s = jnp.where(hit, -(r + 1.0) * RANK_MARK, s)
        vals.append(m)
    rank = jnp.where(s <= -RANK_MARK, s * (-1.0 / RANK_MARK) - 1.0, float(k))
    count = jnp.sum((rank < k).astype(F32), axis=0, keepdims=True)
    return vals, rank, count


def _route_body(hn_ref, wq_ref, k1_ref, k2_ref, e1w_ref, cnt_ref, e2_ref, rk2_ref, qh_ref):
    tm = hn_ref.shape[0]
    qh_ref[...] = _dot_nt(wq_ref[...], hn_ref[...])
    pairs = _pair_candidates()
    n_pad = (-len(pairs)) % 8

    def route_head(h, exact):
        base = pl.multiple_of(h * (2 * HALF_KEY), 2 * HALF_KEY)
        q1h, q1l = _split_bf16(qh_ref[pl.ds(base, HALF_KEY), :])
        q2h, q2l = _split_bf16(qh_ref[pl.ds(base + HALF_KEY, HALF_KEY), :])
        k1h, k1l = _split_bf16(k1_ref[h])
        k2h, k2l = _split_bf16(k2_ref[h])
        dot = functools.partial(jnp.dot, preferred_element_type=F32)
        s1 = dot(k1h, q1h) + (dot(k1h, q1l) + dot(k1l, q1h))
        s2 = dot(k2h, q2h) + (dot(k2h, q2l) + dot(k2l, q2h))
        v1, rk1, n1 = _top_rounds(s1, PEER_TOPK, exact)
        v2, rk2, n2 = _top_rounds(s2, PEER_TOPK, exact)
        cand = jnp.concatenate([v1[r1] + v2[r2] for r1, r2 in pairs]
                               + [jnp.full((n_pad, tm), NEG, F32)], axis=0)
        _, rkc, nc = _top_rounds(cand, PEER_TOPK, exact)
        picked = (rkc < PEER_TOPK).astype(F32)
        ez = jnp.exp(cand - cand[0:1, :]) * picked
        z = jnp.sum(ez, axis=0, keepdims=True)
        cnt = jnp.zeros((N_KEYS, tm), F32)
        for r1 in range(PEER_TOPK):
            rows = [i for i, (a, _) in enumerate(pairs) if a == r1]
            c_r1 = jnp.sum(picked[rows[0]:rows[-1] + 1, :], axis=0, keepdims=True)
            cnt = jnp.where(rk1 == r1, c_r1, cnt)
        e1w_ref[h] = jnp.exp(s1 - v1[0]) / z
        cnt_ref[h] = cnt
        e2_ref[h] = jnp.exp(s2 - v2[0]).astype(BF16)
        rk2_ref[h] = rk2.astype(F32).astype(BF16)
        return jnp.max(jnp.abs(n1 - PEER_TOPK) + jnp.abs(n2 - PEER_TOPK) + jnp.abs(nc - PEER_TOPK))

    def heads(i, carry):
        hs = [i * ROUTE_UNROLL + k for k in range(ROUTE_UNROLL)]
        n_tied = [route_head(h, exact=False) for h in hs]
        for h, n in zip(hs, n_tied):
            @pl.when(n > 0.0)
            def _(h=h):
                route_head(h, exact=True)

        return carry

    lax.fori_loop(0, PEER_HEADS // ROUTE_UNROLL, heads, 0)


def _route(hn, wq_t, keys1, keys2, tm):
    m, d = hn.shape
    nq = wq_t.shape[0]
    out = jax.ShapeDtypeStruct((PEER_HEADS, N_KEYS, m), F32)
    outb = jax.ShapeDtypeStruct((PEER_HEADS, N_KEYS, m), BF16)
    oblk = pl.BlockSpec((PEER_HEADS, N_KEYS, tm), lambda i: (0, 0, i))
    kblk = pl.BlockSpec((PEER_HEADS, N_KEYS, HALF_KEY), lambda i: (0, 0, 0))
    return pl.pallas_call(
        _route_body,
        grid=(m // tm,),
        in_specs=[pl.BlockSpec((tm, d), lambda i: (i, 0)), pl.BlockSpec((nq, d), lambda i: (0, 0)),
                  kblk, kblk],
        out_specs=(oblk, oblk, oblk, oblk),
        out_shape=(out, out, outb, outb),
        scratch_shapes=[pltpu.VMEM((nq, tm), F32)],
        compiler_params=_cparams(("arbitrary",)),
        name="peer_route",
    )(hn, wq_t, keys1, keys2)


def _experts_body(hn_ref, u_ref, vt_ref, e1w_ref, cnt_ref, e2_ref, rk2_ref, h_ref, y_ref, acc_ref, g_ref, *, n_a):
    e = pl.program_id(1)
    n_e = pl.num_programs(1)
    tm = hn_ref.shape[0]
    half_a = n_a // 2

    def build(tile, key1s, chained=True):
        zero = jnp.zeros((), BF16)
        tiles = []
        for al in key1s:
            a = tile * n_a + al
            c16 = [jnp.broadcast_to(cnt_ref[h, pl.ds(a, 1), :], (BF16_ROWS, tm)).astype(BF16)
                   for h in range(PEER_HEADS)]
            w16 = [jnp.broadcast_to(e1w_ref[h, pl.ds(a, 1), :], (BF16_ROWS, tm)).astype(BF16)
                   for h in range(PEER_HEADS)]
            for r in range(0, N_KEYS, BF16_ROWS):
                g = None
                for h in range(PEER_HEADS):
                    t = jnp.where(rk2_ref[h, r:r + BF16_ROWS, :] < c16[h],
                                  e2_ref[h, r:r + BF16_ROWS, :], zero) * w16[h]
                    g = t if g is None else g + t
                tiles.append(g)
                if not chained:
                    continue
                sixteen = jnp.uint32(16)
                bits = pltpu.bitcast(g, jnp.uint32)
                bits = lax.shift_right_logical(lax.shift_right_logical(bits, sixteen), sixteen)
                if len(tiles) % WEIGHT_HOP == 0:
                    bits = pltpu.roll(bits, 1, 1)
                zero = pltpu.bitcast(bits, BF16)
        return jnp.concatenate(tiles, axis=0)

    @pl.when(e == 0)
    def _():
        acc_ref[...] = jnp.zeros_like(acc_ref)
        g_ref[0] = build(0, range(half_a, n_a), chained=False)

    cur = e % 2
    pre = _dot_nt(u_ref[...], hn_ref[...])
    g_own = build(e, range(0, half_a))
    act = (pre * 0.5 * (1.0 + lax.erf(pre * (2.0 ** -0.5)))).astype(BF16)
    p = jnp.concatenate([g_own, g_ref[cur]], axis=0) * act
    acc_ref[...] += jnp.dot(vt_ref[...], p, preferred_element_type=F32)
    g_ref[1 - cur] = build(jnp.minimum(e + 1, n_e - 1), range(half_a, n_a))

    @pl.when(e == n_e - 1)
    def _():
        y_ref[...] = h_ref[...] + acc_ref[...].T


def _experts(hn, u_b, vt_b, e1w, cnt, e2, rk2, h, tm, te):
    m, d = hn.shape
    ne = u_b.shape[0]
    rblk = pl.BlockSpec((PEER_HEADS, N_KEYS, tm), lambda i, e: (0, 0, i))
    return pl.pallas_call(
        functools.partial(_experts_body, n_a=te // N_KEYS),
        grid=(m // tm, ne // te),
        in_specs=[pl.BlockSpec((tm, d), lambda i, e: (i, 0)),
                  pl.BlockSpec((te, d), lambda i, e: (e, 0)),
                  pl.BlockSpec((d, te), lambda i, e: (0, e)),
                  rblk, rblk, rblk, rblk,
                  pl.BlockSpec((tm, d), lambda i, e: (i, 0))],
        out_specs=pl.BlockSpec((tm, d), lambda i, e: (i, 0)),
        out_shape=jax.ShapeDtypeStruct((m, d), F32),
        scratch_shapes=[pltpu.VMEM((d, tm), F32), pltpu.VMEM((2, te // 2, tm), BF16)],
        compiler_params=_cparams(("arbitrary", "arbitrary")),
        name="peer_experts",
    )(hn, u_b, vt_b, e1w, cnt, e2, rk2, h)


PAGES_PER_BLOCK = MOBA_BLOCK // PAGE
SAMPLE_GROUP = 8


def _dot_nn3(a, b):
    ah, al = _split_bf16(a)
    bh, bl = _split_bf16(b)
    dot = functools.partial(jnp.dot, preferred_element_type=F32)
    return dot(ah, bh) + (dot(ah, bl) + dot(al, bh))


def _sample_body(pt_ref, q_ref, kn_ref, vn_ref, *refs, nbp):
    n_pg = SAMPLE_GROUP * PAGES_PER_BLOCK
    k_pages = refs[:n_pg]
    v_pages = refs[n_pg:2 * n_pg]
    o_ref, km_s, m_s, acc_s = refs[2 * n_pg:]
    j = pl.program_id(1)
    dt = q_ref.shape[2]
    grp = SAMPLE_GROUP
    keys = grp * MOBA_BLOCK
    scale = HEAD_DIM ** -0.5
    lane_d = lax.broadcasted_iota(jnp.int32, (HEAD_DIM, LANES), 1)
    lane_t = lax.broadcasted_iota(jnp.int32, (dt, LANES), 1)
    own_blk = (lax.broadcasted_iota(jnp.int32, (grp, keys), 1) // MOBA_BLOCK
               == lax.broadcasted_iota(jnp.int32, (grp, keys), 0))
    ones = jnp.ones((HEAD_DIM, keys), BF16)

    @pl.when(j == 0)
    def _():
        km_s[...] = jnp.zeros_like(km_s)
        m_s[...] = jnp.zeros_like(m_s)

    for h in range(HEADS):
        kt = jnp.concatenate([r[h] for r in k_pages], axis=1)
        vt = jnp.concatenate([r[h] for r in v_pages], axis=1)
        qs = (q_ref[0, h] * scale).astype(BF16)
        s = jnp.dot(qs, kt.astype(BF16), preferred_element_type=F32)
        m = jnp.max(s, axis=-1, keepdims=True)
        p = jnp.exp(s - m)
        p_rows = jnp.concatenate(
            [jnp.where(own_blk, jnp.broadcast_to(p[t:t + 1], (grp, keys)), 0.0) for t in range(dt)],
            axis=0).astype(BF16)
        acc = _dot_nt(p_rows, jnp.concatenate([vt.astype(BF16), ones], axis=0))
        g0 = pl.multiple_of(j * grp, grp)
        for t in range(dt):
            acc_s[h, t, pl.ds(g0, grp), :] = acc[t * grp:(t + 1) * grp]
        m_s[h] = jnp.where(lane_t // grp == j, m, m_s[h])
        km = km_s[h]
        for g in range(grp):
            kmean = jnp.sum(kt[:, g * MOBA_BLOCK:(g + 1) * MOBA_BLOCK], axis=1, keepdims=True)
            km = jnp.where(lane_d == j * grp + g, kmean * (1.0 / MOBA_BLOCK), km)
        km_s[h] = km

    @pl.when(j == pl.num_programs(1) - 1)
    def _():
        tri = (lax.broadcasted_iota(jnp.int32, (dt, dt), 1) <= lax.broadcasted_iota(jnp.int32, (dt, dt), 0))
        own_tok = lax.broadcasted_iota(jnp.int32, (dt, dt * nbp), 1) // nbp == lax.broadcasted_iota(
            jnp.int32, (dt, dt * nbp), 0)
        for h in range(HEADS):
            q = q_ref[0, h]
            gate = jnp.where(lane_t < nbp, _dot_nn3(q, km_s[h]), NEG)
            picked = jnp.zeros(gate.shape, jnp.bool_)
            for _ in range(MOBA_TOPK):
                top = jnp.max(gate, axis=-1, keepdims=True)
                first = jnp.min(jnp.where(gate == top, lane_t, LANES), axis=-1, keepdims=True)
                hit = lane_t == first
                picked = picked | hit
                gate = jnp.where(hit, NEG, gate)
            qs = (q * scale).astype(BF16)
            sn = jnp.where(tri, _dot_nt(qs, kn_ref[0, h].astype(BF16)), NEG)
            mn = jnp.max(sn, axis=-1, keepdims=True)
            pn = jnp.exp(sn - mn)
            ln = jnp.sum(pn, axis=-1, keepdims=True)
            an = jnp.dot(pn.astype(BF16), vn_ref[0, h].astype(BF16), preferred_element_type=F32)
            m_all = m_s[h]
            m_tot = jnp.maximum(jnp.max(jnp.where(picked, m_all, NEG), axis=-1, keepdims=True), mn)
            w = jnp.where(picked, jnp.exp(m_all - m_tot), 0.0)
            wn = jnp.exp(mn - m_tot)
            w_rows = jnp.where(own_tok, jnp.concatenate([w[:, :nbp]] * dt, axis=1), 0.0)
            past = _dot_nn3(w_rows, acc_s[h].reshape(dt * nbp, 2 * HEAD_DIM))
            den = past[:, HEAD_DIM:HEAD_DIM + 1] + wn * ln
            o_ref[0, h] = (past[:, :HEAD_DIM] + wn * an) / den


def _moba_sample(q4, kn4, vn4, cache_k, cache_v, page_table, layer):
    db, _, dt, _ = q4.shape
    nbp = (page_table.shape[1] * PAGE) // MOBA_BLOCK
    assert MOBA_TOPK <= nbp <= LANES and nbp % SAMPLE_GROUP == 0
    kt5 = jnp.transpose(cache_k, (0, 1, 3, 4, 2))
    vt5 = jnp.transpose(cache_v, (0, 1, 3, 4, 2))
    new_blk = pl.BlockSpec((1, HEADS, dt, HEAD_DIM), lambda b, j, pt: (b, 0, 0, 0))
    n_pg = SAMPLE_GROUP * PAGES_PER_BLOCK

    def page_spec(r):
        return pl.BlockSpec((None, None, HEADS, HEAD_DIM, PAGE),
                            lambda b, j, pt: (layer, pt[b, n_pg * j + r], 0, 0, 0))

    pages = [page_spec(r) for r in range(n_pg)]
    return pl.pallas_call(
        functools.partial(_sample_body, nbp=nbp),
        grid_spec=pltpu.PrefetchScalarGridSpec(
            num_scalar_prefetch=1,
            grid=(db, nbp // SAMPLE_GROUP),
            in_specs=[new_blk, new_blk, new_blk] + pages + pages,
            out_specs=new_blk,
            scratch_shapes=[pltpu.VMEM((HEADS, HEAD_DIM, LANES), F32), pltpu.VMEM((HEADS, dt, LANES), F32),
                            pltpu.VMEM((HEADS, dt, nbp, 2 * HEAD_DIM), F32)]),
        out_shape=jax.ShapeDtypeStruct((db, HEADS, dt, HEAD_DIM), F32),
        compiler_params=_cparams(("arbitrary", "arbitrary")),
        name="moba_sample",
    )(page_table, q4, kn4, vn4, *([kt5] * n_pg), *([vt5] * n_pg))


def _mixer_tail(x2d, attn2d, conv2d, w_out_b, g2, wq_t, keys1, keys2, u_b, vt_b, tm, tr, tx, te):
    h, hn = _back(x2d, attn2d, conv2d, w_out_b, g2, tm)
    routing = _route(hn, wq_t, keys1, keys2, tr)
    return _experts(hn, u_b, vt_b, *routing, h, tx, te)


TILE_M = 512
TILE_ROUTE = 256
TILE_X = 512
TILE_E = 1024
CONV_CHUNK = 64


def kernel(x_prompt, x_sample, cache_k, cache_v, state_conv, page_table, norm1_g, w_in, q_norm_g, k_norm_g, conv_w, conv_b, conv_norm_g, w_out, norm2_g, peer_w_query, peer_keys1, peer_keys2, peer_u, peer_v):
    depth = w_in.shape[0]
    bsz, seq, d = x_prompt.shape
    db, dt, _ = x_sample.shape
    assert seq % TILE_M == 0 and TILE_M % MOBA_BLOCK == 0 and d - ATTN_W == ATTN_W
    assert (page_table.shape[1] * PAGE) % MOBA_BLOCK == 0, "partially filled own block is not supported"
    bd = _head_blockdiag()
    xp, xs = x_prompt, x_sample
    rows = {n: [] for n in ("kp", "vp", "cp", "ks", "vs", "cs")}
    for l in range(depth):
        w_in_b = w_in[l].astype(BF16)
        w_out_b = w_out[l].astype(BF16)
        wq_t = peer_w_query[l].T.astype(BF16)
        u_b = peer_u[l].astype(BF16)
        vt_b = peer_v[l].T.astype(BF16)
        g1 = norm1_g[l][None]
        g2 = norm2_g[l][None]
        qg = q_norm_g[l].reshape(1, ATTN_W)
        kg = k_norm_g[l].reshape(1, ATTN_W)
        cb = conv_b[l][None]
        cg = conv_norm_g[l][None]
        tail = functools.partial(_mixer_tail, w_out_b=w_out_b, g2=g2, wq_t=wq_t, keys1=peer_keys1[l],
                                 keys2=peer_keys2[l], u_b=u_b, vt_b=vt_b, te=TILE_E)

        x2d = xp.reshape(bsz * seq, d)
        q, k_t, v_t, u, kb, vb, km = _front(x2d, g1, w_in_b, qg, kg, bd, TILE_M, TILE_M // MOBA_BLOCK, seq=seq)
        r3 = lambda t: t.reshape(bsz, seq, ATTN_W)
        from_t = lambda t: t.reshape(bsz, HEADS, HEAD_DIM, seq).transpose(0, 3, 1, 2)
        attn = _moba_prompt(r3(q), r3(kb), r3(vb), km.reshape(bsz, seq // MOBA_BLOCK, ATTN_W))
        conv = _conv(r3(u), jnp.zeros((bsz, HIST, ATTN_W), F32), conv_w[l], cb, cg, TILE_M, CONV_CHUNK)
        y = tail(x2d, attn.reshape(-1, ATTN_W), conv.reshape(-1, ATTN_W),
                 tm=TILE_M, tr=TILE_ROUTE, tx=TILE_X)
        xp = y.reshape(bsz, seq, d)
        rows["kp"].append(from_t(k_t))
        rows["vp"].append(from_t(v_t))
        rows["cp"].append(r3(u)[:, -(CONV_K - 1):])

        n_s = db * dt
        x2d = xs.reshape(n_s, d)
        q, k, v, u, _, _, _ = _front(x2d, g1, w_in_b, qg, kg, bd, n_s, 1)
        h4 = lambda t: t.reshape(db, dt, HEADS, HEAD_DIM)
        hm = lambda t: h4(t).transpose(0, 2, 1, 3)
        attn = _moba_sample(hm(q), hm(k), hm(v), cache_k, cache_v, page_table, l).transpose(0, 2, 1, 3)
        u3 = u.reshape(db, dt, ATTN_W)
        hist = jnp.concatenate([jnp.zeros((db, HIST - (CONV_K - 1), ATTN_W), F32), state_conv[l]], axis=1)
        conv = _conv(u3, hist, conv_w[l], cb, cg, dt, dt)
        y = tail(x2d, attn.reshape(n_s, ATTN_W), conv.reshape(n_s, ATTN_W), tm=n_s, tr=n_s, tx=n_s)
        xs = y.reshape(db, dt, d)
        rows["ks"].append(h4(k))
        rows["vs"].append(h4(v))
        rows["cs"].append(jnp.concatenate([state_conv[l], u3], axis=1)[:, -(CONV_K - 1):])
    return (xp, xs, jnp.stack(rows["kp"]), jnp.stack(rows["vp"]), jnp.stack(rows["cp"]),
            jnp.stack(rows["ks"]), jnp.stack(rows["vs"]), jnp.stack(rows["cs"]))
```
